```python
import math
import jax, jax.numpy as jnp
from jax import lax
import numpy as np

D_MODEL = 1024
BATCH = 2
SEQ = 8192
DEPTH = 2

N_MIXERS = 2
N_A_LAYERS = (DEPTH + 1) // 2
N_B_LAYERS = DEPTH // 2

HGRN_EXPAND = 128
HGRN_HEADS = D_MODEL // HGRN_EXPAND
HGRN_FDIM = HGRN_HEADS * HGRN_EXPAND
HGRN_IDIM = D_MODEL // HGRN_HEADS
HGRN_CHUNK = 64

ATTN_HEADS = 8
ATTN_HEAD_DIM = D_MODEL // ATTN_HEADS
MOBA_BLOCK = 256
MOBA_TOPK = 3
MOBA_QCHUNK = 32

REL_BUCKETS = 32
REL_MAX_DISTANCE = 1024

D_FF = -(-(8 * D_MODEL) // (3 * 256)) * 256
RMS_EPS = 1e-6

kernel_name = "hgrn2_moba_interleaved_hybrid"


def rms_norm(x, g):
    xf = x.astype(jnp.float32)
    y = xf * lax.rsqrt(jnp.mean(xf * xf, axis=-1, keepdims=True) + RMS_EPS)
    return (y * g.astype(jnp.float32)).astype(x.dtype)


def t5_bucket(dist):
    n = jnp.maximum(dist, 0)
    max_exact = REL_BUCKETS // 2
    nf = jnp.maximum(n, max_exact).astype(jnp.float32)
    large = max_exact + (jnp.log(nf / max_exact) / math.log(REL_MAX_DISTANCE / max_exact)
                         * (REL_BUCKETS - max_exact)).astype(jnp.int32)
    large = jnp.minimum(large, REL_BUCKETS - 1)
    return jnp.where(n < max_exact, n, large)


def hgrn2_mixer(h, w_in, w_out, lb, out_norm):
    b_, s_, _ = h.shape
    f32 = jnp.float32
    proj = h @ w_in
    q, fz, inp, og = jnp.split(proj, [HGRN_FDIM, 2 * HGRN_FDIM, 2 * HGRN_FDIM + D_MODEL], axis=-1)
    q = jax.nn.silu(q.astype(f32))
    lbf = lb.astype(f32)
    f = lbf + (1.0 - lbf) * jax.nn.sigmoid(fz.astype(f32))
    k = 1.0 - f
    logf = jnp.log(f)
    nc = s_ // HGRN_CHUNK

    def to_chunks(t, d):
        return t.reshape(b_, nc, HGRN_CHUNK, HGRN_HEADS, d).transpose(1, 0, 3, 2, 4)

    qc = to_chunks(q, HGRN_EXPAND)
    kc = to_chunks(k, HGRN_EXPAND)
    gc = to_chunks(logf, HGRN_EXPAND)
    ic = to_chunks(inp.astype(f32), HGRN_IDIM)
    causal = jnp.tril(jnp.ones((HGRN_CHUNK, HGRN_CHUNK), dtype=bool))

    def step(state, xs):
        qb, kb, gb, ib = xs
        cum = jnp.cumsum(gb, axis=2)
        diff = cum[:, :, :, None, :] - cum[:, :, None, :, :]
        decay = jnp.exp(jnp.where(causal[:, :, None], diff, -jnp.inf))
        scores = jnp.einsum('bhtn,bhsn,bhtsn->bhts', qb, kb, decay)
        o = (jnp.einsum('bhts,bhsd->bhtd', scores, ib)
             + jnp.einsum('bhtn,bhnd->bhtd', qb * jnp.exp(cum), state))
        last = cum[:, :, -1:, :]
        state = (state * jnp.exp(last[:, :, 0, :, None])
                 + jnp.einsum('bhsn,bhsd->bhnd', kb * jnp.exp(last - cum), ib))
        return state, o

    s0 = jnp.zeros((b_, HGRN_HEADS, HGRN_EXPAND, HGRN_IDIM), f32)
    _, o = lax.scan(step, s0, (qc, kc, gc, ic))
    o = o.transpose(1, 0, 3, 2, 4).reshape(b_, s_, D_MODEL)
    o = rms_norm(o, out_norm) * jax.nn.sigmoid(og.astype(f32))
    return (o.astype(h.dtype) @ w_out).astype(h.dtype)


def moba_mixer(h, w_in, w_out, rel_table):
    b_, s_, _ = h.shape
    f32 = jnp.float32
    nh, dh = ATTN_HEADS, ATTN_HEAD_DIM
    q, k, v = jnp.split(h @ w_in, 3, axis=-1)
    heads = lambda t: t.reshape(b_, s_, nh, dh).transpose(0, 2, 1, 3)
    q, k, v = heads(q), heads(k), heads(v)
    nb = -(-s_ // MOBA_BLOCK)
    pad = nb * MOBA_BLOCK - s_
    kp = jnp.pad(k, ((0, 0), (0, 0), (0, pad), (0, 0)))
    vp = jnp.pad(v, ((0, 0), (0, 0), (0, pad), (0, 0)))
    kb = kp.reshape(b_, nh, nb, MOBA_BLOCK, dh)
    vb = vp.reshape(b_, nh, nb, MOBA_BLOCK, dh)
    kmean = jnp.mean(kb.astype(f32), axis=3)
    scale = dh ** -0.5
    nqc = s_ // MOBA_QCHUNK
    qch = q.reshape(b_, nh, nqc, MOBA_QCHUNK, dh).transpose(2, 0, 1, 3, 4)
    bi = jnp.arange(b_)[:, None, None, None]
    hi = jnp.arange(nh)[None, :, None, None]
    table_t = rel_table.T
    topk = min(MOBA_TOPK, nb)
    kk = topk * MOBA_BLOCK

    def chunk(args):
        c, qb = args
        q0 = c * MOBA_QCHUNK
        qpos = q0 + jnp.arange(MOBA_QCHUNK)
        j = q0 // MOBA_BLOCK
        gate = jnp.einsum('bhqd,bhnd->bhqn', qb.astype(f32), kmean)
        gate = jnp.where(jnp.arange(nb) < j, gate, -jnp.inf)
        _, sel = lax.top_k(gate, topk)
        sel_ok = jnp.arange(topk) < j
        ks = kb[bi, hi, sel]
        vs = vb[bi, hi, sel]
        s_sel = jnp.einsum('bhqd,bhqkpd->bhqkp', qb, ks).astype(f32) * scale
        kpos_sel = sel[..., None] * MOBA_BLOCK + jnp.arange(MOBA_BLOCK)
        bias_sel = table_t[hi[..., None], t5_bucket(qpos[:, None, None] - kpos_sel)]
        s_sel = jnp.where(sel_ok[:, None], s_sel + bias_sel.astype(f32), -jnp.inf)
        kown = lax.dynamic_slice_in_dim(kp, j * MOBA_BLOCK, MOBA_BLOCK, axis=2)
        vown = lax.dynamic_slice_in_dim(vp, j * MOBA_BLOCK, MOBA_BLOCK, axis=2)
        dist_own = qpos[:, None] - (j * MOBA_BLOCK + jnp.arange(MOBA_BLOCK))[None, :]
        bias_own = rel_table[t5_bucket(dist_own)].transpose(2, 0, 1).astype(f32)
        s_own = jnp.einsum('bhqd,bhpd->bhqp', qb, kown).astype(f32) * scale + bias_own
        s_own = jnp.where(dist_own >= 0, s_own, -jnp.inf)
        logits = jnp.concatenate([s_sel.reshape(b_, nh, MOBA_QCHUNK, kk), s_own], axis=-1)
        p = jax.nn.softmax(logits, axis=-1).astype(vb.dtype)
        p_sel = p[..., :kk].reshape(b_, nh, MOBA_QCHUNK, topk, MOBA_BLOCK)
        p_own = p[..., kk:]
        return (jnp.einsum('bhqkp,bhqkpd->bhqd', p_sel, vs)
                + jnp.einsum('bhqp,bhpd->bhqd', p_own, vown))

    o = lax.map(chunk, (jnp.arange(nqc), qch))
    o = o.transpose(1, 0, 3, 2, 4).reshape(b_, s_, D_MODEL)
    return (o @ w_out).astype(h.dtype)


def swiglu(h, w13, w2):
    g, u = jnp.split(h @ w13, 2, axis=-1)
    return ((jax.nn.silu(g) * u) @ w2).astype(h.dtype)


def setup_inputs(seed: int = 0) -> dict:
    key = jax.random.key(seed)
    ks = jax.random.split(key, 14)
    nrm = lambda k, shape, fan_in: jax.random.normal(k, shape, jnp.float32) * (fan_in ** -0.5)
    gain = lambda k, shape: 1.0 + 0.02 * jax.random.normal(k, shape, jnp.float32)
    return {
        "x": jax.random.normal(ks[0], (BATCH, SEQ, D_MODEL), jnp.float32),
        "norm_mix": gain(ks[1], (DEPTH, D_MODEL)),
        "norm_ffn": gain(ks[2], (DEPTH, D_MODEL)),
        "hgrn_w_in": nrm(ks[3], (N_A_LAYERS, D_MODEL, 2 * HGRN_FDIM + 2 * D_MODEL), D_MODEL),
        "hgrn_lb_logits": 1.0 + 0.1 * jax.random.normal(ks[4], (N_A_LAYERS + 1, HGRN_FDIM), jnp.float32),
        "hgrn_out_norm": gain(ks[5], (N_A_LAYERS, D_MODEL)),
        "hgrn_w_out": nrm(ks[6], (N_A_LAYERS, D_MODEL, D_MODEL), D_MODEL),
        "moba_w_in": nrm(ks[7], (N_B_LAYERS, D_MODEL, 3 * D_MODEL), D_MODEL),
        "moba_w_out": nrm(ks[8], (N_B_LAYERS, D_MODEL, D_MODEL), D_MODEL),
        "rel_bias_table": 0.5 * jax.random.normal(ks[9], (REL_BUCKETS, ATTN_HEADS), jnp.float32),
        "ffn_w13": nrm(ks[10], (DEPTH, D_MODEL, 2 * D_FF), D_MODEL),
        "ffn_w2": nrm(ks[11], (DEPTH, D_FF, D_MODEL), D_FF),
        "final_norm": gain(ks[12], (D_MODEL,)),
    }


def reference(x, norm_mix, norm_ffn, hgrn_w_in, hgrn_lb_logits, hgrn_out_norm, hgrn_w_out,
              moba_w_in, moba_w_out, rel_bias_table, ffn_w13, ffn_w2, final_norm):
    lb_all = jnp.cumsum(jax.nn.softmax(hgrn_lb_logits.astype(jnp.float32), axis=0), axis=0)[:N_A_LAYERS]
    h = x
    for layer in range(DEPTH):
        y = rms_norm(h, norm_mix[layer])
        idx = layer // N_MIXERS
        if layer % N_MIXERS == 0:
            y = hgrn2_mixer(y, hgrn_w_in[idx], hgrn_w_out[idx], lb_all[idx], hgrn_out_norm[idx])
        else:
            y = moba_mixer(y, moba_w_in[idx], moba_w_out[idx], rel_bias_table)
        h = h + y
        h = h + swiglu(rms_norm(h, norm_ffn[layer]), ffn_w13[layer], ffn_w2[layer])
    return rms_norm(h, final_norm)
```

```python
import functools
import math

import numpy as np
import jax
import jax.numpy as jnp
from jax import lax
from jax.experimental import pallas as pl
from jax.experimental.pallas import tpu as pltpu

D_MODEL = 1024
HEADS = 8
HEAD_DIM = D_MODEL // HEADS
MOBA_BLOCK = 256
MOBA_TOPK = 3
REL_BUCKETS = 32
REL_MAX_DISTANCE = 1024
RMS_EPS = 1e-6
HGRN_CHUNK = 128
FFN_ROWS = 256
NEG = -1e30
N_BIAS_TILES = 6
VMEM_LIMIT = 56 * 1024 * 1024

BF16 = jnp.bfloat16
F32 = jnp.float32
NT_DIMS = (((1,), (1,)), ((), ()))
TN_DIMS = (((0,), (0,)), ((), ()))


def _resident(shape):
    zeros = (0,) * len(shape)
    return pl.BlockSpec(shape, lambda *_: zeros, pipeline_mode=pl.Buffered(1))


def _sigmoid(x):
    return 1.0 / (1.0 + jnp.exp(-x))


def _rms(x, gain):
    return x * lax.rsqrt(jnp.mean(x * x, axis=-1, keepdims=True) + RMS_EPS) * gain


def _dot(a, b):
    return jnp.dot(a, b, preferred_element_type=F32)


def _hgrn_tables(c):
    ms = [c >> (i + 1) for i in range(int(math.log2(c)))]
    r = np.arange(c)
    level = np.full((c, c), -1, np.int32)
    gsel = np.zeros((len(ms), c, c), np.float32)
    sgn = np.zeros((len(ms), c, HEAD_DIM), np.float32)
    for i, m in enumerate(ms):
        upper = (r % (2 * m)) >= m
        same = (r[:, None] // (2 * m)) == (r[None, :] // (2 * m))
        level[same & upper[:, None] & ~upper[None, :]] = i
        mid = (r // (2 * m)) * 2 * m + m - 1
        gsel[i, r, mid] = 1.0
        sgn[i] = np.where(upper, 1.0, -1.0)[:, None]
    level[r, r] = len(ms)
    tri = (r[:, None] >= r[None, :]).astype(np.float32)
    return (jnp.asarray(tri, BF16), jnp.asarray(gsel, BF16), jnp.asarray(sgn, F32),
            jnp.asarray(level, jnp.int32), len(ms))


def _hgrn_kernel(x_ref, gain_ref, w_ref, lbl_ref, onorm_ref, tri_ref, gsel_ref, sgn_ref, lvl_ref,
                 a_ref, st_ref, o_scr, og_scr, *, layer_idx, n_levels):
    c = x_ref.shape[1]

    @pl.when(pl.program_id(1) == 0)
    def _():
        st_ref[...] = jnp.zeros_like(st_ref)

    xn = _rms(x_ref[0], gain_ref[...]).astype(BF16)
    tri = tri_ref[...]
    lvl = lvl_ref[...]

    def head(h, carry):
        p = _dot(xn, w_ref[h])
        qz, fz, inp, ogz = (p[:, i * HEAD_DIM:(i + 1) * HEAD_DIM] for i in range(4))
        logits = [lbl_ref[r, h] for r in range(lbl_ref.shape[0])]
        top = functools.reduce(jnp.maximum, logits)
        ex = [jnp.exp(t - top) for t in logits]
        lb = sum(ex[:layer_idx + 1]) / sum(ex)
        q = qz * _sigmoid(qz)
        f = lb + (1.0 - lb) * _sigmoid(fz)
        k = 1.0 - f
        g = jnp.log(f)
        inp_b = inp.astype(BF16)

        g0 = g.astype(BF16)
        r1 = g - g0.astype(F32)
        g1 = r1.astype(BF16)
        g2 = (r1 - g1.astype(F32)).astype(BF16)
        cum = _dot(tri, g0) + _dot(tri, g1) + _dot(tri, g2)
        cum_b = cum.astype(BF16)

        scores = jnp.zeros((c, c), F32)
        for i in range(n_levels):
            b = _dot(gsel_ref[i], cum_b)
            e = jnp.exp((cum - b) * sgn_ref[i])
            qe = (q * e).astype(BF16)
            ke = (k * e).astype(BF16)
            prod = lax.dot_general(qe, ke, NT_DIMS, preferred_element_type=F32)
            scores = jnp.where(lvl == i, prod, scores)
        prod = lax.dot_general(q.astype(BF16), k.astype(BF16), NT_DIMS, preferred_element_type=F32)
        scores = jnp.where(lvl == n_levels, prod, scores)
        o = _dot(scores.astype(BF16), inp_b)

        st = st_ref[h]
        qs = (q * jnp.exp(cum)).astype(BF16)
        o = o + lax.dot_general(qs, st.astype(BF16), NT_DIMS, preferred_element_type=F32)
        last = cum[c - 1:c, :]
        kd = (k * jnp.exp(last - cum)).astype(BF16)
        st_ref[h] = st * jnp.exp(last) + lax.dot_general(inp_b, kd, TN_DIMS, preferred_element_type=F32)

        o_scr[h] = o
        og_scr[h] = _sigmoid(ogz)
        return carry

    lax.fori_loop(0, HEADS, head, 0)

    ss = jnp.zeros((c, 1), F32)
    for h in range(HEADS):
        o = o_scr[h]
        ss = ss + jnp.sum(o * o, axis=-1, keepdims=True)
    inv = lax.rsqrt(ss / D_MODEL + RMS_EPS)
    for h in range(HEADS):
        cols = slice(h * HEAD_DIM, (h + 1) * HEAD_DIM)
        a_ref[0, :, cols] = (o_scr[h] * inv * onorm_ref[:, cols] * og_scr[h]).astype(a_ref.dtype)


def _hgrn_mixer(x, gain, w_in, lb_logits, out_norm, layer_idx):
    bsz, seq, _ = x.shape
    c = HGRN_CHUNK
    tri, gsel, sgn, lvl, n_levels = _hgrn_tables(c)
    w = w_in.reshape(D_MODEL, 4, HEADS, HEAD_DIM).transpose(2, 0, 1, 3).reshape(HEADS, D_MODEL, 4 * HEAD_DIM)
    lb_logits = lb_logits.reshape(lb_logits.shape[0], HEADS, 1, HEAD_DIM)
    kern = functools.partial(_hgrn_kernel, layer_idx=layer_idx, n_levels=n_levels)
    return pl.pallas_call(
        kern,
        grid=(bsz, seq // c),
        in_specs=[
            pl.BlockSpec((1, c, D_MODEL), lambda b, i: (b, i, 0)),
            _resident((1, D_MODEL)),
            _resident((HEADS, D_MODEL, 4 * HEAD_DIM)),
            _resident(lb_logits.shape),
            _resident((1, D_MODEL)),
            _resident(tri.shape), _resident(gsel.shape), _resident(sgn.shape), _resident(lvl.shape),
        ],
        out_specs=pl.BlockSpec((1, c, D_MODEL), lambda b, i: (b, i, 0)),
        out_shape=jax.ShapeDtypeStruct((bsz, seq, D_MODEL), BF16),
        scratch_shapes=[
            pltpu.VMEM((HEADS, HEAD_DIM, HEAD_DIM), F32),
            pltpu.VMEM((HEADS, c, HEAD_DIM), F32),
            pltpu.VMEM((HEADS, c, HEAD_DIM), F32),
        ],
        compiler_params=pltpu.CompilerParams(
            dimension_semantics=("arbitrary", "arbitrary"), vmem_limit_bytes=VMEM_LIMIT),
        name="hgrn_mixer",
    )(x, gain.reshape(1, D_MODEL), w.astype(BF16), lb_logits, out_norm.reshape(1, D_MODEL),
      tri, gsel, sgn, lvl)


def _ffn_kernel(res_ref, a_ref, wo_ref, gain_ref, w13_ref, w2_ref, fin_ref, out_ref, *, final_norm):
    d_ff = w2_ref.shape[0]
    h = res_ref[...] + _dot(a_ref[...], wo_ref[...])
    xn = _rms(h, gain_ref[...]).astype(BF16)
    gu = _dot(xn, w13_ref[...])
    g = gu[:, :d_ff]
    u = gu[:, d_ff:]
    act = (g * _sigmoid(g) * u).astype(BF16)
    h = h + _dot(act, w2_ref[...])
    if final_norm:
        h = _rms(h, fin_ref[...])
    out_ref[...] = h


def _outproj_ffn(res, a, w_out, gain, w13, w2, final_gain, final_norm):
    bsz, seq, _ = res.shape
    rows = bsz * seq
    d_ff = w2.shape[0]
    kern = functools.partial(_ffn_kernel, final_norm=final_norm)
    row_spec = pl.BlockSpec((FFN_ROWS, D_MODEL), lambda i: (i, 0))
    out = pl.pallas_call(
        kern,
        grid=(rows // FFN_ROWS,),
        in_specs=[
            row_spec, row_spec,
            _resident((D_MODEL, D_MODEL)),
            _resident((1, D_MODEL)),
            _resident((D_MODEL, 2 * d_ff)),
            _resident((d_ff, D_MODEL)),
            _resident((1, D_MODEL)),
        ],
        out_specs=row_spec,
        out_shape=jax.ShapeDtypeStruct((rows, D_MODEL), F32),
        compiler_params=pltpu.CompilerParams(
            dimension_semantics=("arbitrary",), vmem_limit_bytes=VMEM_LIMIT),
        name="outproj_ffn_final" if final_norm else "outproj_ffn",
    )(res.reshape(rows, D_MODEL), a.reshape(rows, D_MODEL), w_out.astype(BF16), gain.reshape(1, D_MODEL),
      w13.astype(BF16), w2.astype(BF16), final_gain.reshape(1, D_MODEL))
    return out.reshape(bsz, seq, D_MODEL)


def _t5_thresholds():
    max_exact = REL_BUCKETS // 2
    n_log = REL_BUCKETS - max_exact
    ratio = REL_MAX_DISTANCE / max_exact
    return [int(math.ceil(max_exact * ratio ** (k / n_log) - 1e-9)) for k in range(1, n_log)]


def _bias_kernel(tab_ref, out_ref):
    dl = pl.program_id(0)
    tk = lax.broadcasted_iota(jnp.int32, (MOBA_BLOCK, MOBA_BLOCK), 0)
    tq = lax.broadcasted_iota(jnp.int32, (MOBA_BLOCK, MOBA_BLOCK), 1)
    dist = dl * MOBA_BLOCK + tq - tk
    max_exact = REL_BUCKETS // 2
    large = jnp.full(dist.shape, max_exact, jnp.int32)
    for t in _t5_thresholds():
        large = large + (dist >= t).astype(jnp.int32)
    bucket = jnp.where(dist < max_exact, dist, large)
    far = dl == N_BIAS_TILES - 1
    for h in range(HEADS):
        bias = jnp.zeros(dist.shape, F32)
        for b in range(REL_BUCKETS):
            bias = jnp.where(bucket == b, tab_ref[b, h], bias)
        bias = jnp.where(dist < 0, NEG, bias)
        out_ref[0, h] = jnp.where(far, 0.0, bias)


def _bias_tiles(rel_table):
    return pl.pallas_call(
        _bias_kernel,
        grid=(N_BIAS_TILES,),
        in_specs=[pl.BlockSpec(memory_space=pltpu.SMEM)],
        out_specs=pl.BlockSpec((1, HEADS, MOBA_BLOCK, MOBA_BLOCK), lambda i: (i, 0, 0, 0)),
        out_shape=jax.ShapeDtypeStruct((N_BIAS_TILES, HEADS, MOBA_BLOCK, MOBA_BLOCK), F32),
        compiler_params=pltpu.CompilerParams(dimension_semantics=("arbitrary",)),
        name="moba_bias_tiles",
    )(rel_table)


def _moba_proj_kernel(x_ref, gain_ref, w_ref, tab_ref, q_ref, k_ref, vt_ref, mask_ref, kmean_scr):
    j = pl.program_id(1)
    nb = kmean_scr.shape[0]

    @pl.when(j == 0)
    def _():
        kmean_scr[...] = jnp.zeros_like(kmean_scr)

    xn = _rms(x_ref[0], gain_ref[...]).astype(BF16)
    qkv = _dot(xn, w_ref[...])
    q = qkv[:, :D_MODEL]
    k = qkv[:, D_MODEL:2 * D_MODEL]
    v = qkv[:, 2 * D_MODEL:]
    q_ref[0] = (q * (HEAD_DIM ** -0.5)).astype(BF16)
    k_ref[0] = k.astype(BF16)
    vt_ref[0, :, 0] = v.T.reshape(HEADS, HEAD_DIM, MOBA_BLOCK).astype(BF16)

    blk = lax.broadcasted_iota(jnp.int32, (nb, MOBA_BLOCK), 0)
    blk_f = blk.astype(F32)
    for h in range(HEADS):
        cols = slice(h * HEAD_DIM, (h + 1) * HEAD_DIM)
        gate = lax.dot_general(kmean_scr[:, cols], q[:, cols], NT_DIMS,
                               precision=lax.Precision.HIGHEST, preferred_element_type=F32)
        gate = jnp.where(blk < j, gate, -jnp.inf)
        madd = jnp.full(gate.shape, NEG, F32)
        far_bias = jnp.where(j - blk >= N_BIAS_TILES - 1, tab_ref[REL_BUCKETS - 1:REL_BUCKETS, h:h + 1], 0.0)
        for _ in range(MOBA_TOPK):
            top = jnp.max(gate, axis=0, keepdims=True)
            idx = jnp.min(jnp.where(gate == top, blk_f, float(nb)), axis=0, keepdims=True)
            hit = blk_f == idx
            madd = jnp.where(hit & (top > -jnp.inf), far_bias, madd)
            gate = jnp.where(hit, -jnp.inf, gate)
        mask_ref[0, h] = jnp.where(blk == j, 0.0, madd)

    kmean_scr[pl.ds(j, 1), :] = jnp.mean(k, axis=0, keepdims=True)


def _moba_proj(h, gain, w_in, rel_table):
    bsz, seq, _ = h.shape
    nb = seq // MOBA_BLOCK
    return pl.pallas_call(
        _moba_proj_kernel,
        grid=(bsz, nb),
        in_specs=[
            pl.BlockSpec((1, MOBA_BLOCK, D_MODEL), lambda b, j: (b, j, 0)),
            _resident((1, D_MODEL)),
            _resident((D_MODEL, 3 * D_MODEL)),
            _resident(rel_table.shape),
        ],
        out_specs=[
            pl.BlockSpec((1, MOBA_BLOCK, D_MODEL), lambda b, j: (b, j, 0)),
            pl.BlockSpec((1, MOBA_BLOCK, D_MODEL), lambda b, j: (b, j, 0)),
            pl.BlockSpec((1, HEADS, 1, HEAD_DIM, MOBA_BLOCK), lambda b, j: (b, 0, j, 0, 0)),
            pl.BlockSpec((1, HEADS, nb, MOBA_BLOCK), lambda b, j: (b, 0, 0, j)),
        ],
        out_shape=[
            jax.ShapeDtypeStruct((bsz, seq, D_MODEL), BF16),
            jax.ShapeDtypeStruct((bsz, seq, D_MODEL), BF16),
            jax.ShapeDtypeStruct((bsz, HEADS, nb, HEAD_DIM, MOBA_BLOCK), BF16),
            jax.ShapeDtypeStruct((bsz, HEADS, nb, seq), F32),
        ],
        scratch_shapes=[pltpu.VMEM((nb, D_MODEL), F32)],
        compiler_params=pltpu.CompilerParams(
            dimension_semantics=("arbitrary", "arbitrary"), vmem_limit_bytes=VMEM_LIMIT),
        name="moba_proj",
    )(h, gain.reshape(1, D_MODEL), w_in.astype(BF16), rel_table)


def _moba_attn_kernel(q_ref, k_ref, vt_ref, mask_ref, bias_ref, out_ref):
    j = pl.program_id(2)
    q = q_ref[0]

    def block(i, carry):
        m, l, acc = carry
        n = j - i
        rows = pl.ds(pl.multiple_of(n * MOBA_BLOCK, MOBA_BLOCK), MOBA_BLOCK)
        s = lax.dot_general(k_ref[0, rows, :], q, NT_DIMS, preferred_element_type=F32)
        s = s + bias_ref[jnp.minimum(i, N_BIAS_TILES - 1), 0] + mask_ref[0, 0, pl.ds(n, 1), :]
        m_new = jnp.maximum(m, jnp.max(s, axis=0, keepdims=True))
        alpha = jnp.exp(m - m_new)
        p = jnp.exp(s - m_new)
        l = alpha * l + jnp.sum(p, axis=0, keepdims=True)
        acc = alpha * acc + _dot(vt_ref[0, 0, n], p.astype(BF16))
        return m_new, l, acc

    init = (jnp.full((1, MOBA_BLOCK), NEG, F32), jnp.zeros((1, MOBA_BLOCK), F32),
            jnp.zeros((HEAD_DIM, MOBA_BLOCK), F32))
    _, l, acc = lax.fori_loop(0, j + 1, block, init)
    out_ref[0] = (acc / l).T.astype(out_ref.dtype)


def _moba_attn(q, k, vt, mask, bias):
    bsz, seq, _ = q.shape
    nb = seq // MOBA_BLOCK
    return pl.pallas_call(
        _moba_attn_kernel,
        grid=(bsz, HEADS, nb),
        in_specs=[
            pl.BlockSpec((1, MOBA_BLOCK, HEAD_DIM), lambda b, h, j: (b, j, h)),
            pl.BlockSpec((1, seq, HEAD_DIM), lambda b, h, j: (b, 0, h)),
            pl.BlockSpec((1, 1, nb, HEAD_DIM, MOBA_BLOCK), lambda b, h, j: (b, h, 0, 0, 0)),
            pl.BlockSpec((1, 1, nb, MOBA_BLOCK), lambda b, h, j: (b, h, 0, j)),
            pl.BlockSpec((N_BIAS_TILES, 1, MOBA_BLOCK, MOBA_BLOCK), lambda b, h, j: (0, h, 0, 0)),
        ],
        out_specs=pl.BlockSpec((1, MOBA_BLOCK, HEAD_DIM), lambda b, h, j: (b, j, h)),
        out_shape=jax.ShapeDtypeStruct((bsz, seq, D_MODEL), BF16),
        compiler_params=pltpu.CompilerParams(
            dimension_semantics=("arbitrary", "arbitrary", "arbitrary"), vmem_limit_bytes=VMEM_LIMIT),
        name="moba_attn",
    )(q, k, vt, mask, bias)


def kernel(x, norm_mix, norm_ffn, hgrn_w_in, hgrn_lb_logits, hgrn_out_norm, hgrn_w_out,
           moba_w_in, moba_w_out, rel_bias_table, ffn_w13, ffn_w2, final_norm):
    depth = norm_mix.shape[0]
    n_mixers = 2
    bias = _bias_tiles(rel_bias_table)
    h = x
    for layer in range(depth):
        idx = layer // n_mixers
        if layer % n_mixers == 0:
            a = _hgrn_mixer(h, norm_mix[layer], hgrn_w_in[idx], hgrn_lb_logits, hgrn_out_norm[idx], idx)
            w_out = hgrn_w_out[idx]
        else:
            q, k, vt, mask = _moba_proj(h, norm_mix[layer], moba_w_in[idx], rel_bias_table)
            a = _moba_attn(q, k, vt, mask, bias)
            w_out = moba_w_out[idx]
        h = _outproj_ffn(h, a, w_out, norm_ffn[layer], ffn_w13[layer], ffn_w2[layer], final_norm,
                         final_norm=(layer == depth - 1))
    return h
```

```python
import functools
import math

import numpy as np
import jax
import jax.numpy as jnp
from jax import lax
from jax.experimental import pallas as pl
from jax.experimental.pallas import tpu as pltpu

D_MODEL = 1024
HEADS = 8
HEAD_DIM = D_MODEL // HEADS
MOBA_BLOCK = 256
MOBA_TOPK = 3
REL_BUCKETS = 32
REL_MAX_DISTANCE = 1024
RMS_EPS = 1e-6
HGRN_CHUNK = 128
FFN_ROWS = 256
ATTN_GROUP = 8
NEG = -1e30
N_BIAS_TILES = 6
VMEM_LIMIT = 56 * 1024 * 1024
LOG2E = math.log2(math.e)
SUBLANES = 8

BF16 = jnp.bfloat16
F32 = jnp.float32
NT_DIMS = (((1,), (1,)), ((), ()))
TN_DIMS = (((0,), (0,)), ((), ()))


def _resident(shape):
    zeros = (0,) * len(shape)
    return pl.BlockSpec(shape, lambda *_: zeros, pipeline_mode=pl.Buffered(1))


def _sigmoid(x):
    return 1.0 / (1.0 + jnp.exp(-x))


def _rms(x, gain):
    return x * lax.rsqrt(jnp.mean(x * x, axis=-1, keepdims=True) + RMS_EPS) * gain


def _dot(a, b):
    return jnp.dot(a, b, preferred_element_type=F32)


def _dot_nt(a, b):
    return lax.dot_general(a, b, NT_DIMS, preferred_element_type=F32)


def _hgrn_tables(c):
    ms = [c >> (i + 1) for i in range(int(math.log2(c)))]
    r = np.arange(c)
    level = np.full((c, c), -1, np.int32)
    gsel = np.zeros((len(ms), c, c), np.float32)
    sgn = np.zeros((len(ms), c, HEAD_DIM), np.float32)
    for i, m in enumerate(ms):
        upper = (r % (2 * m)) >= m
        same = (r[:, None] // (2 * m)) == (r[None, :] // (2 * m))
        level[same & upper[:, None] & ~upper[None, :]] = i
        mid = (r // (2 * m)) * 2 * m + m - 1
        gsel[i, r, mid] = 1.0
        sgn[i] = np.where(upper, 1.0, -1.0)[:, None]
    level[r, r] = len(ms)
    tri = (r[:, None] >= r[None, :]).astype(np.float32)
    return (jnp.asarray(tri, BF16), jnp.asarray(gsel.reshape(len(ms) * c, c), BF16), jnp.asarray(sgn, F32),
            jnp.asarray(level, jnp.int32), len(ms))


def _hgrn_kernel(x_ref, gain_ref, w_ref, lbl_ref, onorm_ref, tri_ref, gsel_ref, sgn_ref, lvl_ref,
                 a_ref, st_ref, *, layer_idx, n_levels):
    c = x_ref.shape[1]
    d = D_MODEL

    @pl.when(pl.program_id(1) == 0)
    def _():
        st_ref[...] = jnp.zeros_like(st_ref)

    xn = _rms(x_ref[0], gain_ref[...]).astype(BF16)
    p = _dot(xn, w_ref[...])

    logits = [lbl_ref[r:r + 1, :] for r in range(lbl_ref.shape[0])]
    top = functools.reduce(jnp.maximum, logits)
    ex = [jnp.exp(t - top) for t in logits]
    lb = sum(ex[:layer_idx + 1]) / sum(ex)

    qz = p[:, :d]
    q = qz * _sigmoid(qz)
    f = lb + (1.0 - lb) * _sigmoid(p[:, d:2 * d])
    k = 1.0 - f
    g = jnp.log(f) * LOG2E
    inp_b = p[:, 2 * d:3 * d].astype(BF16)
    og = _sigmoid(p[:, 3 * d:])

    tri = tri_ref[...]
    g0 = g.astype(BF16)
    r1 = g - g0.astype(F32)
    g1 = r1.astype(BF16)
    g2 = (r1 - g1.astype(F32)).astype(BF16)
    cum = _dot(tri, g0) + _dot(tri, g1) + _dot(tri, g2)
    refs = _dot(gsel_ref[...], cum.astype(BF16))

    qe, ke = [], []
    for i in range(n_levels):
        e = jnp.exp2((cum - refs[i * c:(i + 1) * c]) * jnp.tile(sgn_ref[i], (1, HEADS)))
        qe.append((q * e).astype(BF16))
        ke.append((k * e).astype(BF16))
    qe.append(q.astype(BF16))
    ke.append(k.astype(BF16))

    last = cum[c - 1:c, :]
    qs = (q * jnp.exp2(cum)).astype(BF16)
    kd = (k * jnp.exp2(last - cum)).astype(BF16)
    decay = jnp.exp2(last)

    lvl = lvl_ref[...]
    outs = []
    for h in range(HEADS):
        cols = slice(h * HEAD_DIM, (h + 1) * HEAD_DIM)
        scores = jnp.zeros((c, c), F32)
        for i in range(n_levels + 1):
            scores = jnp.where(lvl == i, _dot_nt(qe[i][:, cols], ke[i][:, cols]), scores)
        st = st_ref[h]
        o = _dot(scores.astype(BF16), inp_b[:, cols]) + _dot_nt(qs[:, cols], st.astype(BF16))
        st_ref[h] = st * decay[:, cols] + lax.dot_general(inp_b[:, cols], kd[:, cols], TN_DIMS,
                                                          preferred_element_type=F32)
        outs.append(o)

    ss = sum(jnp.sum(o * o, axis=-1, keepdims=True) for o in outs)
    inv = lax.rsqrt(ss / D_MODEL + RMS_EPS)
    for h in range(HEADS):
        cols = slice(h * HEAD_DIM, (h + 1) * HEAD_DIM)
        a_ref[0, :, cols] = (outs[h] * inv * onorm_ref[:, cols] * og[:, cols]).astype(a_ref.dtype)


def _hgrn_mixer(x, gain, w_in, lb_logits, out_norm, layer_idx):
    bsz, seq, _ = x.shape
    c = HGRN_CHUNK
    tri, gsel, sgn, lvl, n_levels = _hgrn_tables(c)
    kern = functools.partial(_hgrn_kernel, layer_idx=layer_idx, n_levels=n_levels)
    return pl.pallas_call(
        kern,
        grid=(bsz, seq // c),
        in_specs=[
            pl.BlockSpec((1, c, D_MODEL), lambda b, i: (b, i, 0)),
            _resident((1, D_MODEL)),
            _resident(w_in.shape),
            _resident(lb_logits.shape),
            _resident((1, D_MODEL)),
            _resident(tri.shape), _resident(gsel.shape), _resident(sgn.shape), _resident(lvl.shape),
        ],
        out_specs=pl.BlockSpec((1, c, D_MODEL), lambda b, i: (b, i, 0)),
        out_shape=jax.ShapeDtypeStruct((bsz, seq, D_MODEL), BF16),
        scratch_shapes=[pltpu.VMEM((HEADS, HEAD_DIM, HEAD_DIM), F32)],
        compiler_params=pltpu.CompilerParams(
            dimension_semantics=("arbitrary", "arbitrary"), vmem_limit_bytes=VMEM_LIMIT),
        name="hgrn_mixer",
    )(x, gain.reshape(1, D_MODEL), w_in.astype(BF16), lb_logits, out_norm.reshape(1, D_MODEL),
      tri, gsel, sgn, lvl)


def _ffn_kernel(res_ref, a_ref, wo_ref, gain_ref, w13_ref, w2_ref, fin_ref, out_ref, *, final_norm):
    d_ff = w2_ref.shape[0]
    h = res_ref[...] + _dot(a_ref[...], wo_ref[...])
    xn = _rms(h, gain_ref[...]).astype(BF16)
    gu = _dot(xn, w13_ref[...])
    g = gu[:, :d_ff]
    u = gu[:, d_ff:]
    act = (g * _sigmoid(g) * u).astype(BF16)
    h = h + _dot(act, w2_ref[...])
    if final_norm:
        h = _rms(h, fin_ref[...])
    out_ref[...] = h


def _outproj_ffn(res, a, w_out, gain, w13, w2, final_gain, final_norm):
    bsz, seq, _ = res.shape
    rows = bsz * seq
    d_ff = w2.shape[0]
    kern = functools.partial(_ffn_kernel, final_norm=final_norm)
    row_spec = pl.BlockSpec((FFN_ROWS, D_MODEL), lambda i: (i, 0))
    out = pl.pallas_call(
        kern,
        grid=(rows // FFN_ROWS,),
        in_specs=[
            row_spec, row_spec,
            _resident((D_MODEL, D_MODEL)),
            _resident((1, D_MODEL)),
            _resident((D_MODEL, 2 * d_ff)),
            _resident((d_ff, D_MODEL)),
            _resident((1, D_MODEL)),
        ],
        out_specs=row_spec,
        out_shape=jax.ShapeDtypeStruct((rows, D_MODEL), F32),
        compiler_params=pltpu.CompilerParams(
            dimension_semantics=("arbitrary",), vmem_limit_bytes=VMEM_LIMIT),
        name="outproj_ffn_final" if final_norm else "outproj_ffn",
    )(res.reshape(rows, D_MODEL), a.reshape(rows, D_MODEL), w_out.astype(BF16), gain.reshape(1, D_MODEL),
      w13.astype(BF16), w2.astype(BF16), final_gain.reshape(1, D_MODEL))
    return out.reshape(bsz, seq, D_MODEL)


def _t5_thresholds():
    max_exact = REL_BUCKETS // 2
    n_log = REL_BUCKETS - max_exact
    ratio = REL_MAX_DISTANCE / max_exact
    return [int(math.ceil(max_exact * ratio ** (k / n_log) - 1e-9)) for k in range(1, n_log)]


def _bias_kernel(tab_ref, out_ref):
    dl = pl.program_id(0)
    tk = lax.broadcasted_iota(jnp.int32, (MOBA_BLOCK, MOBA_BLOCK), 0)
    tq = lax.broadcasted_iota(jnp.int32, (MOBA_BLOCK, MOBA_BLOCK), 1)
    dist = dl * MOBA_BLOCK + tq - tk
    max_exact = REL_BUCKETS // 2
    large = jnp.full(dist.shape, max_exact, jnp.int32)
    for t in _t5_thresholds():
        large = large + (dist >= t).astype(jnp.int32)
    bucket = jnp.where(dist < max_exact, dist, large)
    far = dl == N_BIAS_TILES - 1
    for h in range(HEADS):
        bias = jnp.zeros(dist.shape, F32)
        for b in range(REL_BUCKETS):
            bias = jnp.where(bucket == b, tab_ref[b, h] * LOG2E, bias)
        bias = jnp.where(dist < 0, NEG, bias)
        out_ref[0, h] = jnp.where(far, 0.0, bias)


def _bias_tiles(rel_table):
    return pl.pallas_call(
        _bias_kernel,
        grid=(N_BIAS_TILES,),
        in_specs=[pl.BlockSpec(memory_space=pltpu.SMEM)],
        out_specs=pl.BlockSpec((1, HEADS, MOBA_BLOCK, MOBA_BLOCK), lambda i: (i, 0, 0, 0)),
        out_shape=jax.ShapeDtypeStruct((N_BIAS_TILES, HEADS, MOBA_BLOCK, MOBA_BLOCK), F32),
        compiler_params=pltpu.CompilerParams(dimension_semantics=("arbitrary",)),
        name="moba_bias_tiles",
    )(rel_table)


def _moba_proj_kernel(x_ref, gain_ref, w_ref, tab_ref, qa_ref, ka_ref, vt_ref, kmean_scr):
    j = pl.program_id(1)
    nb = kmean_scr.shape[0]
    width = 2 * HEAD_DIM

    @pl.when(j == 0)
    def _():
        kmean_scr[...] = jnp.zeros_like(kmean_scr)

    xn = _rms(x_ref[0], gain_ref[...]).astype(BF16)
    qkv = _dot(xn, w_ref[...])
    q = qkv[:, :D_MODEL]
    k = qkv[:, D_MODEL:2 * D_MODEL]
    v = qkv[:, 2 * D_MODEL:]
    vt_ref[0, :, 0] = v.T.reshape(HEADS, HEAD_DIM, MOBA_BLOCK).astype(BF16)

    lane = lax.broadcasted_iota(jnp.int32, (MOBA_BLOCK, HEAD_DIM), 1)
    onehot = ((lane == j) | (lane == j + nb)).astype(BF16)
    blk = lax.broadcasted_iota(jnp.int32, (nb, MOBA_BLOCK), 0)
    blk_f = blk.astype(F32)
    pad = jnp.zeros((HEAD_DIM - 2 * nb, MOBA_BLOCK), F32)
    for h in range(HEADS):
        cols = slice(h * HEAD_DIM, (h + 1) * HEAD_DIM)
        gate = lax.dot_general(kmean_scr[:, cols], q[:, cols], NT_DIMS,
                               precision=lax.Precision.HIGHEST, preferred_element_type=F32)
        gate = jnp.where(blk < j, gate, -jnp.inf)
        madd = jnp.full(gate.shape, NEG, F32)
        far_bias = jnp.where(j - blk >= N_BIAS_TILES - 1,
                             tab_ref[REL_BUCKETS - 1:REL_BUCKETS, h:h + 1] * LOG2E, 0.0)
        for _ in range(MOBA_TOPK):
            top = jnp.max(gate, axis=0, keepdims=True)
            idx = jnp.min(jnp.where(gate == top, blk_f, float(nb)), axis=0, keepdims=True)
            hit = blk_f == idx
            madd = jnp.where(hit & (top > -jnp.inf), far_bias, madd)
            gate = jnp.where(hit, -jnp.inf, gate)
        madd = jnp.where(blk == j, 0.0, madd)
        hi = madd.astype(BF16).astype(F32)
        lo = (madd - hi).astype(BF16).astype(F32)
        qt = (q[:, cols] * (LOG2E * HEAD_DIM ** -0.5)).T
        qa_ref[0, h] = jnp.concatenate([qt, hi, lo, pad], axis=0).astype(BF16)
        ka_ref[0, :, h * width:h * width + HEAD_DIM] = k[:, cols].astype(BF16)
        ka_ref[0, :, h * width + HEAD_DIM:(h + 1) * width] = onehot

    kmean_scr[pl.ds(j, 1), :] = jnp.mean(k, axis=0, keepdims=True)


def _moba_proj(h, gain, w_in, rel_table):
    bsz, seq, _ = h.shape
    nb = seq // MOBA_BLOCK
    assert 2 * nb <= HEAD_DIM
    aug_spec = pl.BlockSpec((1, MOBA_BLOCK, 2 * D_MODEL), lambda b, j: (b, j, 0))
    return pl.pallas_call(
        _moba_proj_kernel,
        grid=(bsz, nb),
        in_specs=[
            pl.BlockSpec((1, MOBA_BLOCK, D_MODEL), lambda b, j: (b, j, 0)),
            _resident((1, D_MODEL)),
            _resident((D_MODEL, 3 * D_MODEL)),
            _resident(rel_table.shape),
        ],
        out_specs=[
            pl.BlockSpec((1, HEADS, 2 * HEAD_DIM, MOBA_BLOCK), lambda b, j: (b, 0, 0, j)),
            aug_spec,
            pl.BlockSpec((1, HEADS, 1, HEAD_DIM, MOBA_BLOCK), lambda b, j: (b, 0, j, 0, 0)),
        ],
        out_shape=[
            jax.ShapeDtypeStruct((bsz, HEADS, 2 * HEAD_DIM, seq), BF16),
            jax.ShapeDtypeStruct((bsz, seq, 2 * D_MODEL), BF16),
            jax.ShapeDtypeStruct((bsz, HEADS, nb, HEAD_DIM, MOBA_BLOCK), BF16),
        ],
        scratch_shapes=[pltpu.VMEM((nb, D_MODEL), F32)],
        compiler_params=pltpu.CompilerParams(
            dimension_semantics=("arbitrary", "arbitrary"), vmem_limit_bytes=VMEM_LIMIT),
        name="moba_proj",
    )(h, gain.reshape(1, D_MODEL), w_in.astype(BF16), rel_table)


def _moba_attn_kernel(qa_ref, ka_ref, vt_ref, bias_ref, out_ref, s_scr):
    j = pl.program_id(2)
    qa = qa_ref[0, 0]
    groups = (j + ATTN_GROUP) // ATTN_GROUP
    far_groups = jnp.maximum(j - (N_BIAS_TILES - 2), 0) // ATTN_GROUP
    fold = (MOBA_BLOCK // SUBLANES, SUBLANES, MOBA_BLOCK)

    def scores(n):
        rows = pl.ds(pl.multiple_of(n * MOBA_BLOCK, MOBA_BLOCK), MOBA_BLOCK)
        return _dot(ka_ref[0, rows, :], qa)

    def far_group(g, mrun):
        for u in range(ATTN_GROUP):
            n = g * ATTN_GROUP + u
            s = scores(n)
            s_scr[n] = s
            mrun = jnp.maximum(mrun, jnp.max(s.reshape(fold), axis=0))
        return mrun

    def near_group(g, mrun):
        for u in range(ATTN_GROUP):
            n = g * ATTN_GROUP + u
            s = scores(n) + bias_ref[jnp.clip(j - n, 0, N_BIAS_TILES - 1), 0]
            s_scr[n] = s
            mrun = jnp.maximum(mrun, jnp.max(s.reshape(fold), axis=0))
        return mrun

    mrun = jnp.full((SUBLANES, MOBA_BLOCK), NEG, F32)
    mrun = lax.fori_loop(0, far_groups, far_group, mrun)
    mrun = lax.fori_loop(far_groups, groups, near_group, mrun)
    m = jnp.max(mrun, axis=0, keepdims=True)

    def pv_group(g, carry):
        lrun, acc = carry
        for u in range(ATTN_GROUP):
            n = g * ATTN_GROUP + u
            p = jnp.exp2(s_scr[n] - m)
            lrun = lrun + jnp.sum(p.reshape(fold), axis=0)
            acc = acc + _dot(vt_ref[0, 0, n], p.astype(BF16))
        return lrun, acc

    init = (jnp.zeros((SUBLANES, MOBA_BLOCK), F32), jnp.zeros((HEAD_DIM, MOBA_BLOCK), F32))
    lrun, acc = lax.fori_loop(0, groups, pv_group, init)
    l = jnp.sum(lrun, axis=0, keepdims=True)
    out_ref[0] = (acc / l).T.astype(out_ref.dtype)


def _moba_attn(qa, ka, vt, bias):
    bsz, seq, _ = ka.shape
    nb = seq // MOBA_BLOCK
    assert nb % ATTN_GROUP == 0
    width = 2 * HEAD_DIM
    return pl.pallas_call(
        _moba_attn_kernel,
        grid=(bsz, HEADS, nb),
        in_specs=[
            pl.BlockSpec((1, 1, width, MOBA_BLOCK), lambda b, h, j: (b, h, 0, j)),
            pl.BlockSpec((1, seq, width), lambda b, h, j: (b, 0, h)),
            pl.BlockSpec((1, 1, nb, HEAD_DIM, MOBA_BLOCK), lambda b, h, j: (b, h, 0, 0, 0)),
            pl.BlockSpec((N_BIAS_TILES, 1, MOBA_BLOCK, MOBA_BLOCK), lambda b, h, j: (0, h, 0, 0)),
        ],
        out_specs=pl.BlockSpec((1, MOBA_BLOCK, HEAD_DIM), lambda b, h, j: (b, j, h)),
        out_shape=jax.ShapeDtypeStruct((bsz, seq, D_MODEL), BF16),
        scratch_shapes=[pltpu.VMEM((nb, MOBA_BLOCK, MOBA_BLOCK), F32)],
        compiler_params=pltpu.CompilerParams(
            dimension_semantics=("arbitrary", "arbitrary", "arbitrary"), vmem_limit_bytes=VMEM_LIMIT),
        name="moba_attn",
    )(qa, ka, vt, bias)


def kernel(x, norm_mix, norm_ffn, hgrn_w_in, hgrn_lb_logits, hgrn_out_norm, hgrn_w_out,
           moba_w_in, moba_w_out, rel_bias_table, ffn_w13, ffn_w2, final_norm):
    depth = norm_mix.shape[0]
    n_mixers = 2
    bias = _bias_tiles(rel_bias_table)
    h = x
    for layer in range(depth):
        idx = layer // n_mixers
        if layer % n_mixers == 0:
            a = _hgrn_mixer(h, norm_mix[layer], hgrn_w_in[idx], hgrn_lb_logits, hgrn_out_norm[idx], idx)
            w_out = hgrn_w_out[idx]
        else:
            qa, ka, vt = _moba_proj(h, norm_mix[layer], moba_w_in[idx], rel_bias_table)
            a = _moba_attn(qa, ka, vt, bias)
            w_out = moba_w_out[idx]
        h = _outproj_ffn(h, a, w_out, norm_ffn[layer], ffn_w13[layer], ffn_w2[layer], final_norm,
                         final_norm=(layer == depth - 1))
    return h
```

```python
import functools
import math

import numpy as np
import jax
import jax.numpy as jnp
from jax import lax
from jax.experimental import pallas as pl
from jax.experimental.pallas import tpu as pltpu

D_MODEL = 1024
HEADS = 8
HEAD_DIM = D_MODEL // HEADS
MOBA_BLOCK = 256
MOBA_TOPK = 3
REL_BUCKETS = 32
REL_MAX_DISTANCE = 1024
RMS_EPS = 1e-6
HGRN_CHUNK = 128
FFN_ROWS = 512
ATTN_GROUP = 4
NEG = -1e30
N_BIAS_TILES = 6
VMEM_LIMIT = 56 * 1024 * 1024
LOG2E = math.log2(math.e)
SUBLANES = 8

BF16 = jnp.bfloat16
F32 = jnp.float32
NT_DIMS = (((1,), (1,)), ((), ()))
TN_DIMS = (((0,), (0,)), ((), ()))


def _resident(shape):
    zeros = (0,) * len(shape)
    return pl.BlockSpec(shape, lambda *_: zeros, pipeline_mode=pl.Buffered(1))


def _sigmoid(x):
    return 1.0 / (1.0 + jnp.exp(-x))


def _rms(x, gain):
    return x * lax.rsqrt(jnp.mean(x * x, axis=-1, keepdims=True) + RMS_EPS) * gain


def _dot(a, b):
    return jnp.dot(a, b, preferred_element_type=F32)


def _dot_nt(a, b):
    return lax.dot_general(a, b, NT_DIMS, preferred_element_type=F32)


def _run_staggered(sequences):
    live = list(enumerate(sequences))
    step = 0
    while live:
        for i, seq in list(live):
            if step >= i and next(seq, StopIteration) is StopIteration:
                live.remove((i, seq))
        step += 1


def _hgrn_tables(c):
    ms = [c >> (i + 1) for i in range(int(math.log2(c)))]
    r = np.arange(c)
    level = np.full((c, c), -1, np.int32)
    gsel = np.zeros((len(ms), c, c), np.float32)
    sgn = np.zeros((len(ms), c, HEAD_DIM), np.float32)
    for i, m in enumerate(ms):
        upper = (r % (2 * m)) >= m
        same = (r[:, None] // (2 * m)) == (r[None, :] // (2 * m))
        level[same & upper[:, None] & ~upper[None, :]] = i
        mid = (r // (2 * m)) * 2 * m + m - 1
        gsel[i, r, mid] = 1.0
        sgn[i] = np.where(upper, 1.0, -1.0)[:, None]
    level[r, r] = len(ms)
    tri = (r[:, None] >= r[None, :]).astype(np.float32)
    return (jnp.asarray(tri, BF16), jnp.asarray(gsel.reshape(len(ms) * c, c), BF16), jnp.asarray(sgn, F32),
            jnp.asarray(level, jnp.int32), len(ms))


def _hgrn_kernel(x_ref, gain_ref, w_ref, lbl_ref, onorm_ref, tri_ref, gsel_ref, sgn_ref, lvl_ref,
                 a_ref, st_ref, *, layer_idx, n_levels):
    bsz, c, d = x_ref.shape

    @pl.when(pl.program_id(0) == 0)
    def _():
        st_ref[...] = jnp.zeros_like(st_ref)

    logits = [lbl_ref[r:r + 1, :] for r in range(lbl_ref.shape[0])]
    top = functools.reduce(jnp.maximum, logits)
    ex = [jnp.exp(t - top) for t in logits]
    lb = sum(ex[:layer_idx + 1]) / sum(ex)

    xn = _rms(x_ref[...].reshape(bsz * c, d), gain_ref[...]).astype(BF16)
    p_all = _dot(xn, w_ref[...])

    def chunk(b):
        p = p_all[b * c:(b + 1) * c]
        qz = p[:, :d]
        q = qz * _sigmoid(qz)
        f = lb + (1.0 - lb) * _sigmoid(p[:, d:2 * d])
        k = 1.0 - f
        g = jnp.log(f) * LOG2E
        inp_b = p[:, 2 * d:3 * d].astype(BF16)
        og = _sigmoid(p[:, 3 * d:])
        g0 = g.astype(BF16)
        r1 = g - g0.astype(F32)
        g1 = r1.astype(BF16)
        g2 = (r1 - g1.astype(F32)).astype(BF16)
        yield
        tri = tri_ref[...]
        cum = _dot(tri, g0) + _dot(tri, g1) + _dot(tri, g2)
        refs = _dot(gsel_ref[...], cum.astype(BF16))
        yield
        qe, ke = [], []
        for i in range(n_levels):
            e = jnp.exp2((cum - refs[i * c:(i + 1) * c]) * jnp.tile(sgn_ref[i], (1, HEADS)))
            qe.append((q * e).astype(BF16))
            ke.append((k * e).T.astype(BF16))
        qe.append(q.astype(BF16))
        ke.append(k.T.astype(BF16))
        last = cum[c - 1:c, :]
        qs = (q * jnp.exp2(cum)).astype(BF16)
        kd = (k * jnp.exp2(last - cum)).T.astype(BF16)
        decay = jnp.broadcast_to(jnp.exp2(last), (SUBLANES, d)).T[:, 0:1]
        yield
        lvl = lvl_ref[...]
        outs = []
        for h in range(HEADS):
            cols = slice(h * HEAD_DIM, (h + 1) * HEAD_DIM)
            scores = jnp.zeros((c, c), F32)
            for i in range(n_levels + 1):
                scores = jnp.where(lvl == i, _dot(qe[i][:, cols], ke[i][cols, :]), scores)
            st = st_ref[b, h]
            o = _dot(scores.astype(BF16), inp_b[:, cols]) + _dot(qs[:, cols], st.astype(BF16))
            st_ref[b, h] = st * decay[cols, :] + _dot(kd[cols, :], inp_b[:, cols])
            outs.append(o)
        yield
        ss = sum(jnp.sum(o * o, axis=-1, keepdims=True) for o in outs)
        inv = lax.rsqrt(ss / D_MODEL + RMS_EPS)
        for h in range(HEADS):
            cols = slice(h * HEAD_DIM, (h + 1) * HEAD_DIM)
            a_ref[b, :, cols] = (outs[h] * inv * onorm_ref[:, cols] * og[:, cols]).astype(a_ref.dtype)

    _run_staggered([chunk(b) for b in range(bsz)])


def _hgrn_mixer(x, gain, w_in, lb_logits, out_norm, layer_idx):
    bsz, seq, _ = x.shape
    c = HGRN_CHUNK
    tri, gsel, sgn, lvl, n_levels = _hgrn_tables(c)
    kern = functools.partial(_hgrn_kernel, layer_idx=layer_idx, n_levels=n_levels)
    return pl.pallas_call(
        kern,
        grid=(seq // c,),
        in_specs=[
            pl.BlockSpec((bsz, c, D_MODEL), lambda i: (0, i, 0)),
            _resident((1, D_MODEL)),
            _resident(w_in.shape),
            _resident(lb_logits.shape),
            _resident((1, D_MODEL)),
            _resident(tri.shape), _resident(gsel.shape), _resident(sgn.shape), _resident(lvl.shape),
        ],
        out_specs=pl.BlockSpec((bsz, c, D_MODEL), lambda i: (0, i, 0)),
        out_shape=jax.ShapeDtypeStruct((bsz, seq, D_MODEL), BF16),
        scratch_shapes=[pltpu.VMEM((bsz, HEADS, HEAD_DIM, HEAD_DIM), F32)],
        compiler_params=pltpu.CompilerParams(
            dimension_semantics=("arbitrary",), vmem_limit_bytes=VMEM_LIMIT),
        name="hgrn_mixer",
    )(x, gain.reshape(1, D_MODEL), w_in.astype(BF16), lb_logits, out_norm.reshape(1, D_MODEL),
      tri, gsel, sgn, lvl)


def _ffn_kernel(res_ref, a_ref, wo_ref, gain_ref, w13_ref, w2_ref, fin_ref, out_ref, *, final_norm):
    d_ff = w2_ref.shape[0]
    h = res_ref[...] + _dot(a_ref[...], wo_ref[...])
    xn = _rms(h, gain_ref[...]).astype(BF16)
    gu = _dot(xn, w13_ref[...])
    g = gu[:, :d_ff]
    u = gu[:, d_ff:]
    act = (g * _sigmoid(g) * u).astype(BF16)
    h = h + _dot(act, w2_ref[...])
    if final_norm:
        h = _rms(h, fin_ref[...])
    out_ref[...] = h


def _outproj_ffn(res, a, w_out, gain, w13, w2, final_gain, final_norm):
    bsz, seq, _ = res.shape
    rows = bsz * seq
    d_ff = w2.shape[0]
    kern = functools.partial(_ffn_kernel, final_norm=final_norm)
    row_spec = pl.BlockSpec((FFN_ROWS, D_MODEL), lambda i: (i, 0))
    out = pl.pallas_call(
        kern,
        grid=(rows // FFN_ROWS,),
        in_specs=[
            row_spec, row_spec,
            _resident((D_MODEL, D_MODEL)),
            _resident((1, D_MODEL)),
            _resident((D_MODEL, 2 * d_ff)),
            _resident((d_ff, D_MODEL)),
            _resident((1, D_MODEL)),
        ],
        out_specs=row_spec,
        out_shape=jax.ShapeDtypeStruct((rows, D_MODEL), F32),
        compiler_params=pltpu.CompilerParams(
            dimension_semantics=("arbitrary",), vmem_limit_bytes=VMEM_LIMIT),
        name="outproj_ffn_final" if final_norm else "outproj_ffn",
    )(res.reshape(rows, D_MODEL), a.reshape(rows, D_MODEL), w_out.astype(BF16), gain.reshape(1, D_MODEL),
      w13.astype(BF16), w2.astype(BF16), final_gain.reshape(1, D_MODEL))
    return out.reshape(bsz, seq, D_MODEL)


def _t5_thresholds():
    max_exact = REL_BUCKETS // 2
    n_log = REL_BUCKETS - max_exact
    ratio = REL_MAX_DISTANCE / max_exact
    return [int(math.ceil(max_exact * ratio ** (k / n_log) - 1e-9)) for k in range(1, n_log)]


def _bias_kernel(tab_ref, out_ref):
    dl = pl.program_id(0)
    tk = lax.broadcasted_iota(jnp.int32, (MOBA_BLOCK, MOBA_BLOCK), 0)
    tq = lax.broadcasted_iota(jnp.int32, (MOBA_BLOCK, MOBA_BLOCK), 1)
    dist = dl * MOBA_BLOCK + tq - tk
    max_exact = REL_BUCKETS // 2
    large = jnp.full(dist.shape, max_exact, jnp.int32)
    for t in _t5_thresholds():
        large = large + (dist >= t).astype(jnp.int32)
    bucket = jnp.where(dist < max_exact, dist, large)
    far = dl == N_BIAS_TILES - 1
    for h in range(HEADS):
        bias = jnp.zeros(dist.shape, F32)
        for b in range(REL_BUCKETS):
            bias = jnp.where(bucket == b, tab_ref[b, h] * LOG2E, bias)
        bias = jnp.where(dist < 0, NEG, bias)
        out_ref[0, h] = jnp.where(far, 0.0, bias)


def _bias_tiles(rel_table):
    return pl.pallas_call(
        _bias_kernel,
        grid=(N_BIAS_TILES,),
        in_specs=[pl.BlockSpec(memory_space=pltpu.SMEM)],
        out_specs=pl.BlockSpec((1, HEADS, MOBA_BLOCK, MOBA_BLOCK), lambda i: (i, 0, 0, 0)),
        out_shape=jax.ShapeDtypeStruct((N_BIAS_TILES, HEADS, MOBA_BLOCK, MOBA_BLOCK), F32),
        compiler_params=pltpu.CompilerParams(dimension_semantics=("arbitrary",)),
        name="moba_bias_tiles",
    )(rel_table)


def _moba_proj_kernel(x_ref, gain_ref, w_ref, tab_ref, qa_ref, ka_ref, vt_ref, kmean_scr):
    j = pl.program_id(1)
    nb = kmean_scr.shape[0]
    width = 2 * HEAD_DIM

    @pl.when(j == 0)
    def _():
        kmean_scr[...] = jnp.zeros_like(kmean_scr)

    xn = _rms(x_ref[0], gain_ref[...]).astype(BF16)
    qkv = _dot(xn, w_ref[...])
    q = qkv[:, :D_MODEL]
    k = qkv[:, D_MODEL:2 * D_MODEL]
    v = qkv[:, 2 * D_MODEL:]
    vt_ref[0, :, 0] = v.T.reshape(HEADS, HEAD_DIM, MOBA_BLOCK).astype(BF16)

    lane = lax.broadcasted_iota(jnp.int32, (MOBA_BLOCK, HEAD_DIM), 1)
    onehot = ((lane == j) | (lane == j + nb)).astype(BF16)
    blk = lax.broadcasted_iota(jnp.int32, (nb, MOBA_BLOCK), 0)
    blk_f = blk.astype(F32)
    pad = jnp.zeros((HEAD_DIM - 2 * nb, MOBA_BLOCK), F32)
    for h in range(HEADS):
        cols = slice(h * HEAD_DIM, (h + 1) * HEAD_DIM)
        gate = lax.dot_general(kmean_scr[:, cols], q[:, cols], NT_DIMS,
                               precision=lax.Precision.HIGHEST, preferred_element_type=F32)
        gate = jnp.where(blk < j, gate, -jnp.inf)
        madd = jnp.full(gate.shape, NEG, F32)
        far_bias = jnp.where(j - blk >= N_BIAS_TILES - 1,
                             tab_ref[REL_BUCKETS - 1:REL_BUCKETS, h:h + 1] * LOG2E, 0.0)
        for _ in range(MOBA_TOPK):
            top = jnp.max(gate, axis=0, keepdims=True)
            idx = jnp.min(jnp.where(gate == top, blk_f, float(nb)), axis=0, keepdims=True)
            hit = blk_f == idx
            madd = jnp.where(hit & (top > -jnp.inf), far_bias, madd)
            gate = jnp.where(hit, -jnp.inf, gate)
        madd = jnp.where(blk == j, 0.0, madd)
        hi = madd.astype(BF16).astype(F32)
        lo = (madd - hi).astype(BF16).astype(F32)
        qt = (q[:, cols] * (LOG2E * HEAD_DIM ** -0.5)).T
        qa_ref[0, h] = jnp.concatenate([qt, hi, lo, pad], axis=0).astype(BF16)
        ka_ref[0, :, h * width:h * width + HEAD_DIM] = k[:, cols].astype(BF16)
        ka_ref[0, :, h * width + HEAD_DIM:(h + 1) * width] = onehot

    kmean_scr[pl.ds(j, 1), :] = jnp.mean(k, axis=0, keepdims=True)


def _moba_proj(h, gain, w_in, rel_table):
    bsz, seq, _ = h.shape
    nb = seq // MOBA_BLOCK
    assert 2 * nb <= HEAD_DIM
    aug_spec = pl.BlockSpec((1, MOBA_BLOCK, 2 * D_MODEL), lambda b, j: (b, j, 0))
    return pl.pallas_call(
        _moba_proj_kernel,
        grid=(bsz, nb),
        in_specs=[
            pl.BlockSpec((1, MOBA_BLOCK, D_MODEL), lambda b, j: (b, j, 0)),
            _resident((1, D_MODEL)),
            _resident((D_MODEL, 3 * D_MODEL)),
            _resident(rel_table.shape),
        ],
        out_specs=[
            pl.BlockSpec((1, HEADS, 2 * HEAD_DIM, MOBA_BLOCK), lambda b, j: (b, 0, 0, j)),
            aug_spec,
            pl.BlockSpec((1, HEADS, 1, HEAD_DIM, MOBA_BLOCK), lambda b, j: (b, 0, j, 0, 0)),
        ],
        out_shape=[
            jax.ShapeDtypeStruct((bsz, HEADS, 2 * HEAD_DIM, seq), BF16),
            jax.ShapeDtypeStruct((bsz, seq, 2 * D_MODEL), BF16),
            jax.ShapeDtypeStruct((bsz, HEADS, nb, HEAD_DIM, MOBA_BLOCK), BF16),
        ],
        scratch_shapes=[pltpu.VMEM((nb, D_MODEL), F32)],
        compiler_params=pltpu.CompilerParams(
            dimension_semantics=("arbitrary", "arbitrary"), vmem_limit_bytes=VMEM_LIMIT),
        name="moba_proj",
    )(h, gain.reshape(1, D_MODEL), w_in.astype(BF16), rel_table)


def _attn_tile(qa_ref, ka_ref, vt_ref, bias_ref, out_ref, b, j, n_groups):
    qa = qa_ref[b, 0]
    fold = (MOBA_BLOCK // SUBLANES, SUBLANES, MOBA_BLOCK)
    first_near = (n_groups - 2) * ATTN_GROUP

    def qk_group(g):
        tiles = []
        for n in range(g * ATTN_GROUP, (g + 1) * ATTN_GROUP):
            s = _dot(ka_ref[b, n * MOBA_BLOCK:(n + 1) * MOBA_BLOCK, :], qa)
            if n >= first_near:
                s = s + bias_ref[jnp.clip(j - n, 0, N_BIAS_TILES - 1), 0]
            tiles.append(s)
        return tiles

    m = jnp.full((1, MOBA_BLOCK), NEG, F32)
    lrun = jnp.zeros((SUBLANES, MOBA_BLOCK), F32)
    acc = jnp.zeros((HEAD_DIM, MOBA_BLOCK), F32)
    tiles = qk_group(0)
    yield
    for g in range(n_groups):
        nxt = qk_group(g + 1) if g + 1 < n_groups else None
        yield
        gmax = functools.reduce(jnp.maximum, [jnp.max(s.reshape(fold), axis=0) for s in tiles])
        m_new = jnp.maximum(m, jnp.max(gmax, axis=0, keepdims=True))
        alpha = jnp.exp2(m - m_new)
        lsum = jnp.zeros((SUBLANES, MOBA_BLOCK), F32)
        pv = jnp.zeros((HEAD_DIM, MOBA_BLOCK), F32)
        for u, s in enumerate(tiles):
            p = jnp.exp2(s - m_new)
            lsum = lsum + jnp.sum(p.reshape(fold), axis=0)
            pv = pv + _dot(vt_ref[b, 0, g * ATTN_GROUP + u], p.astype(BF16))
        lrun = alpha * lrun + lsum
        acc = alpha * acc + pv
        m = m_new
        tiles = nxt
        yield
    l = jnp.sum(lrun, axis=0, keepdims=True)
    out_ref[b] = (acc / l).T.astype(out_ref.dtype)


def _moba_attn_kernel(qa_ref, ka_ref, vt_ref, bias_ref, out_ref):
    j = pl.program_id(1)
    bsz = qa_ref.shape[0]
    nb = vt_ref.shape[2]
    for n_groups in range(1, nb // ATTN_GROUP + 1):
        @pl.when(j // ATTN_GROUP == n_groups - 1)
        def _():
            _run_staggered([_attn_tile(qa_ref, ka_ref, vt_ref, bias_ref, out_ref, b, j, n_groups)
                            for b in range(bsz)])


def _moba_attn(qa, ka, vt, bias):
    bsz, seq, _ = ka.shape
    nb = seq // MOBA_BLOCK
    assert nb % ATTN_GROUP == 0
    width = 2 * HEAD_DIM
    return pl.pallas_call(
        _moba_attn_kernel,
        grid=(HEADS, nb),
        in_specs=[
            pl.BlockSpec((bsz, 1, width, MOBA_BLOCK), lambda h, j: (0, h, 0, j)),
            pl.BlockSpec((bsz, seq, width), lambda h, j: (0, 0, h)),
            pl.BlockSpec((bsz, 1, nb, HEAD_DIM, MOBA_BLOCK), lambda h, j: (0, h, 0, 0, 0)),
            pl.BlockSpec((N_BIAS_TILES, 1, MOBA_BLOCK, MOBA_BLOCK), lambda h, j: (0, h, 0, 0)),
        ],
        out_specs=pl.BlockSpec((bsz, MOBA_BLOCK, HEAD_DIM), lambda h, j: (0, j, h)),
        out_shape=jax.ShapeDtypeStruct((bsz, seq, D_MODEL), BF16),
        compiler_params=pltpu.CompilerParams(
            dimension_semantics=("arbitrary", "arbitrary"), vmem_limit_bytes=VMEM_LIMIT),
        name="moba_attn",
    )(qa, ka, vt, bias)


def kernel(x, norm_mix, norm_ffn, hgrn_w_in, hgrn_lb_logits, hgrn_out_norm, hgrn_w_out,
           moba_w_in, moba_w_out, rel_bias_table, ffn_w13, ffn_w2, final_norm):
    depth = norm_mix.shape[0]
    n_mixers = 2
    bias = _bias_tiles(rel_bias_table)
    h = x
    for layer in range(depth):
        idx = layer // n_mixers
        if layer % n_mixers == 0:
            a = _hgrn_mixer(h, norm_mix[layer], hgrn_w_in[idx], hgrn_lb_logits, hgrn_out_norm[idx], idx)
            w_out = hgrn_w_out[idx]
        else:
            qa, ka, vt = _moba_proj(h, norm_mix[layer], moba_w_in[idx], rel_bias_table)
            a = _moba_attn(qa, ka, vt, bias)
            w_out = moba_w_out[idx]
        h = _outproj_ffn(h, a, w_out, norm_ffn[layer], ffn_w13[layer], ffn_w2[layer], final_norm,
                         final_norm=(layer == depth - 1))
    return h
```

```python
import functools
import math

import numpy as np
import jax
import jax.numpy as jnp
from jax import lax
from jax.experimental import pallas as pl
from jax.experimental.pallas import tpu as pltpu

D_MODEL = 1024
HEADS = 8
HEAD_DIM = D_MODEL // HEADS
MOBA_BLOCK = 256
MOBA_TOPK = 3
REL_BUCKETS = 32
REL_MAX_DISTANCE = 1024
RMS_EPS = 1e-6
HGRN_CHUNK = 128
FFN_ROWS = 512
ATTN_GROUP = 4
NEG = -1e30
N_BIAS_TILES = 6
VMEM_LIMIT = 56 * 1024 * 1024
LOG2E = math.log2(math.e)
SUBLANES = 8

BF16 = jnp.bfloat16
F32 = jnp.float32
NT_DIMS = (((1,), (1,)), ((), ()))


def _resident(shape):
    zeros = (0,) * len(shape)
    return pl.BlockSpec(shape, lambda *_: zeros, pipeline_mode=pl.Buffered(1))


def _sigmoid(x):
    return 1.0 / (1.0 + jnp.exp(-x))


def _rms(x, gain):
    return x * lax.rsqrt(jnp.mean(x * x, axis=-1, keepdims=True) + RMS_EPS) * gain


def _dot(a, b):
    return jnp.dot(a, b, preferred_element_type=F32)


def _run_staggered(sequences):
    live = list(enumerate(sequences))
    step = 0
    while live:
        for i, seq in list(live):
            if step >= i and next(seq, StopIteration) is StopIteration:
                live.remove((i, seq))
        step += 1


def _hgrn_tables(c):
    ms = [c >> (i + 1) for i in range(int(math.log2(c)))]
    r = np.arange(c)
    level = np.full((c, c), -1, np.int32)
    gsel = np.zeros((len(ms), c, c), np.float32)
    sgn = np.zeros((len(ms), c, HEAD_DIM), np.float32)
    for i, m in enumerate(ms):
        upper = (r % (2 * m)) >= m
        same = (r[:, None] // (2 * m)) == (r[None, :] // (2 * m))
        level[same & upper[:, None] & ~upper[None, :]] = i
        mid = (r // (2 * m)) * 2 * m + m - 1
        gsel[i, r, mid] = 1.0
        sgn[i] = np.where(upper, 1.0, -1.0)[:, None]
    level[r, r] = len(ms)
    tri = (r[:, None] >= r[None, :]).astype(np.float32)
    fine = [i for i, m in enumerate(ms) if 2 * m < SUBLANES]
    return (jnp.asarray(tri, BF16), jnp.asarray(gsel[fine].reshape(len(fine) * c, c), BF16), jnp.asarray(sgn, F32),
            jnp.asarray(level, jnp.int32), ms)


def _hgrn_kernel(x_ref, gain_ref, w_ref, lbl_ref, onorm_ref, tri_ref, gsel_ref, sgn_ref, lvl_ref,
                 a_ref, st_ref, *, layer_idx, ms):
    bsz, c, d = x_ref.shape
    n_levels = len(ms)

    @pl.when(pl.program_id(0) == 0)
    def _():
        st_ref[...] = jnp.zeros_like(st_ref)

    logits = [lbl_ref[r:r + 1, :] for r in range(lbl_ref.shape[0])]
    top = functools.reduce(jnp.maximum, logits)
    ex = [jnp.exp(t - top) for t in logits]
    lb = sum(ex[:layer_idx + 1]) / sum(ex)

    xn = _rms(x_ref[...].reshape(bsz * c, d), gain_ref[...]).astype(BF16)
    p_all = _dot(xn, w_ref[...])

    def chunk(b):
        p = p_all[b * c:(b + 1) * c]
        qz = p[:, :d]
        q = qz * _sigmoid(qz)
        f = lb + (1.0 - lb) * _sigmoid(p[:, d:2 * d])
        k = 1.0 - f
        g = jnp.log(f) * LOG2E
        inp_b = p[:, 2 * d:3 * d].astype(BF16)
        og = _sigmoid(p[:, 3 * d:])
        g0 = g.astype(BF16)
        r1 = g - g0.astype(F32)
        g1 = r1.astype(BF16)
        g2 = (r1 - g1.astype(F32)).astype(BF16)
        yield
        tri = tri_ref[...]
        cum = _dot(tri, g0) + _dot(tri, g1) + _dot(tri, g2)
        fine_refs = _dot(gsel_ref[...], cum.astype(BF16))
        yield
        qe, ke = [], []
        for i, m in enumerate(ms):
            if 2 * m >= SUBLANES:
                blocks = cum.reshape(c // (2 * m), 2 * m, d)
                ref = jnp.broadcast_to(blocks[:, m - 1:m, :], blocks.shape).reshape(c, d)
            else:
                fi = i - (n_levels - gsel_ref.shape[0] // c)
                ref = fine_refs[fi * c:(fi + 1) * c]
            e = jnp.exp2((cum - ref) * jnp.tile(sgn_ref[i], (1, HEADS)))
            qe.append((q * e).astype(BF16))
            ke.append((k * e).T.astype(BF16))
        qe.append(q.astype(BF16))
        ke.append(k.T.astype(BF16))
        last = cum[c - 1:c, :]
        qs = (q * jnp.exp2(cum)).astype(BF16)
        kd = (k * jnp.exp2(last - cum)).T.astype(BF16)
        decay = jnp.broadcast_to(jnp.exp2(last), (SUBLANES, d)).T[:, 0:1]
        yield
        lvl = lvl_ref[...]
        outs = []
        for h in range(HEADS):
            cols = slice(h * HEAD_DIM, (h + 1) * HEAD_DIM)
            scores = jnp.zeros((c, c), F32)
            for i in range(n_levels + 1):
                scores = jnp.where(lvl == i, _dot(qe[i][:, cols], ke[i][cols, :]), scores)
            st = st_ref[b, h]
            o = _dot(scores.astype(BF16), inp_b[:, cols]) + _dot(qs[:, cols], st.astype(BF16))
            st_ref[b, h] = st * decay[cols, :] + _dot(kd[cols, :], inp_b[:, cols])
            outs.append(o)
        yield
        ss = sum(jnp.sum(o * o, axis=-1, keepdims=True) for o in outs)
        inv = lax.rsqrt(ss / D_MODEL + RMS_EPS)
        for h in range(HEADS):
            cols = slice(h * HEAD_DIM, (h + 1) * HEAD_DIM)
            a_ref[b, :, cols] = (outs[h] * inv * onorm_ref[:, cols] * og[:, cols]).astype(a_ref.dtype)

    _run_staggered([chunk(b) for b in range(bsz)])


def _hgrn_mixer(x, gain, w_in, lb_logits, out_norm, layer_idx):
    bsz, seq, _ = x.shape
    c = HGRN_CHUNK
    tri, gsel, sgn, lvl, ms = _hgrn_tables(c)
    kern = functools.partial(_hgrn_kernel, layer_idx=layer_idx, ms=tuple(ms))
    return pl.pallas_call(
        kern,
        grid=(seq // c,),
        in_specs=[
            pl.BlockSpec((bsz, c, D_MODEL), lambda i: (0, i, 0)),
            _resident((1, D_MODEL)),
            _resident(w_in.shape),
            _resident(lb_logits.shape),
            _resident((1, D_MODEL)),
            _resident(tri.shape), _resident(gsel.shape), _resident(sgn.shape), _resident(lvl.shape),
        ],
        out_specs=pl.BlockSpec((bsz, c, D_MODEL), lambda i: (0, i, 0)),
        out_shape=jax.ShapeDtypeStruct((bsz, seq, D_MODEL), BF16),
        scratch_shapes=[pltpu.VMEM((bsz, HEADS, HEAD_DIM, HEAD_DIM), F32)],
        compiler_params=pltpu.CompilerParams(
            dimension_semantics=("arbitrary",), vmem_limit_bytes=VMEM_LIMIT),
        name="hgrn_mixer",
    )(x, gain.reshape(1, D_MODEL), w_in.astype(BF16), lb_logits, out_norm.reshape(1, D_MODEL),
      tri, gsel, sgn, lvl)


def _ffn_kernel(res_ref, a_ref, wo_ref, gain_ref, w13_ref, w2_ref, fin_ref, out_ref, *, final_norm):
    d_ff = w2_ref.shape[0]
    h = res_ref[...] + _dot(a_ref[...], wo_ref[...])
    xn = _rms(h, gain_ref[...]).astype(BF16)
    gu = _dot(xn, w13_ref[...])
    g = gu[:, :d_ff]
    u = gu[:, d_ff:]
    act = (g * _sigmoid(g) * u).astype(BF16)
    h = h + _dot(act, w2_ref[...])
    if final_norm:
        h = _rms(h, fin_ref[...])
    out_ref[...] = h


def _outproj_ffn(res, a, w_out, gain, w13, w2, final_gain, final_norm):
    bsz, seq, _ = res.shape
    rows = bsz * seq
    d_ff = w2.shape[0]
    kern = functools.partial(_ffn_kernel, final_norm=final_norm)
    row_spec = pl.BlockSpec((FFN_ROWS, D_MODEL), lambda i: (i, 0))
    out = pl.pallas_call(
        kern,
        grid=(rows // FFN_ROWS,),
        in_specs=[
            row_spec, row_spec,
            _resident((D_MODEL, D_MODEL)),
            _resident((1, D_MODEL)),
            _resident((D_MODEL, 2 * d_ff)),
            _resident((d_ff, D_MODEL)),
            _resident((1, D_MODEL)),
        ],
        out_specs=row_spec,
        out_shape=jax.ShapeDtypeStruct((rows, D_MODEL), F32),
        compiler_params=pltpu.CompilerParams(
            dimension_semantics=("arbitrary",), vmem_limit_bytes=VMEM_LIMIT),
        name="outproj_ffn_final" if final_norm else "outproj_ffn",
    )(res.reshape(rows, D_MODEL), a.reshape(rows, D_MODEL), w_out.astype(BF16), gain.reshape(1, D_MODEL),
      w13.astype(BF16), w2.astype(BF16), final_gain.reshape(1, D_MODEL))
    return out.reshape(bsz, seq, D_MODEL)


def _t5_thresholds():
    max_exact = REL_BUCKETS // 2
    n_log = REL_BUCKETS - max_exact
    ratio = REL_MAX_DISTANCE / max_exact
    return [int(math.ceil(max_exact * ratio ** (k / n_log) - 1e-9)) for k in range(1, n_log)]


def _t5_bucket(dist):
    max_exact = REL_BUCKETS // 2
    return dist if dist < max_exact else max_exact + sum(dist >= t for t in _t5_thresholds())


def _bias_kernel(tab_ref, out_ref):
    dl = pl.program_id(0)
    tk = lax.broadcasted_iota(jnp.int32, (MOBA_BLOCK, MOBA_BLOCK), 0)
    tq = lax.broadcasted_iota(jnp.int32, (MOBA_BLOCK, MOBA_BLOCK), 1)
    max_exact = REL_BUCKETS // 2
    for tile in range(N_BIAS_TILES - 1):
        @pl.when(dl == tile)
        def _():
            dist = tile * MOBA_BLOCK + tq - tk
            d_lo = max(tile * MOBA_BLOCK - (MOBA_BLOCK - 1), 0)
            d_hi = tile * MOBA_BLOCK + MOBA_BLOCK - 1
            b_lo, b_hi = _t5_bucket(d_lo), _t5_bucket(d_hi)
            large = jnp.full(dist.shape, max_exact, jnp.int32)
            for t in _t5_thresholds():
                if t <= d_hi:
                    large = large + (dist >= t).astype(jnp.int32)
            bucket = jnp.where(dist < max_exact, dist, large) if d_lo < max_exact else large
            for h in range(HEADS):
                bias = jnp.full(dist.shape, tab_ref[b_lo, h] * LOG2E, F32)
                for bkt in range(b_lo + 1, b_hi + 1):
                    bias = jnp.where(bucket == bkt, tab_ref[bkt, h] * LOG2E, bias)
                if tile == 0:
                    bias = jnp.where(dist < 0, NEG, bias)
                out_ref[0, h] = bias

    @pl.when(dl == N_BIAS_TILES - 1)
    def _():
        out_ref[...] = jnp.zeros_like(out_ref)


def _bias_tiles(rel_table):
    return pl.pallas_call(
        _bias_kernel,
        grid=(N_BIAS_TILES,),
        in_specs=[pl.BlockSpec(memory_space=pltpu.SMEM)],
        out_specs=pl.BlockSpec((1, HEADS, MOBA_BLOCK, MOBA_BLOCK), lambda i: (i, 0, 0, 0)),
        out_shape=jax.ShapeDtypeStruct((N_BIAS_TILES, HEADS, MOBA_BLOCK, MOBA_BLOCK), F32),
        compiler_params=pltpu.CompilerParams(dimension_semantics=("arbitrary",)),
        name="moba_bias_tiles",
    )(rel_table)


def _moba_proj_kernel(x_ref, gain_ref, w_ref, tab_ref, qa_ref, ka_ref, vt_ref, kmean_scr):
    j = pl.program_id(0)
    bsz, nb = kmean_scr.shape[0], kmean_scr.shape[2]
    width = 2 * HEAD_DIM
    col_head = lax.broadcasted_iota(jnp.int32, (1, D_MODEL), 1) // HEAD_DIM

    @pl.when(j == 0)
    def _():
        kmean_scr[...] = jnp.zeros_like(kmean_scr)

    xn = _rms(x_ref[...].reshape(bsz * MOBA_BLOCK, D_MODEL), gain_ref[...]).astype(BF16)
    qkv = _dot(xn, w_ref[...])

    lane = lax.broadcasted_iota(jnp.int32, (MOBA_BLOCK, HEAD_DIM), 1)
    onehot = ((lane == j) | (lane == j + nb)).astype(BF16)
    blk = lax.broadcasted_iota(jnp.int32, (nb, MOBA_BLOCK), 0)
    blk_f = blk.astype(F32)
    pad = jnp.zeros((HEAD_DIM - 2 * nb, MOBA_BLOCK), F32)
    for b in range(bsz):
        rows = slice(b * MOBA_BLOCK, (b + 1) * MOBA_BLOCK)
        q = qkv[rows, :D_MODEL]
        k = qkv[rows, D_MODEL:2 * D_MODEL]
        v = qkv[rows, 2 * D_MODEL:]
        vt_ref[b, :, 0] = v.T.reshape(HEADS, HEAD_DIM, MOBA_BLOCK).astype(BF16)
        qt_all = q.T
        km = kmean_scr[b].reshape(HEADS * nb, D_MODEL)
        km_hi = km.astype(BF16)
        km_lo = (km - km_hi.astype(F32)).astype(BF16)
        qt_hi = qt_all.astype(BF16)
        qt_lo = (qt_all - qt_hi.astype(F32)).astype(BF16)
        gates = _dot(km_hi, qt_hi) + _dot(km_hi, qt_lo) + _dot(km_lo, qt_hi)
        for h in range(HEADS):
            cols = slice(h * HEAD_DIM, (h + 1) * HEAD_DIM)
            gate = jnp.where(blk < j, gates[h * nb:(h + 1) * nb], -jnp.inf)
            madd = jnp.full(gate.shape, NEG, F32)
            far_bias = jnp.where(j - blk >= N_BIAS_TILES - 1,
                                 tab_ref[REL_BUCKETS - 1:REL_BUCKETS, h:h + 1] * LOG2E, 0.0)
            for _ in range(MOBA_TOPK):
                top = jnp.max(gate, axis=0, keepdims=True)
                idx = jnp.min(jnp.where(gate == top, blk_f, float(nb)), axis=0, keepdims=True)
                hit = blk_f == idx
                madd = jnp.where(hit & (top > -jnp.inf), far_bias, madd)
                gate = jnp.where(hit, -jnp.inf, gate)
            madd = jnp.where(blk == j, 0.0, madd)
            hi = madd.astype(BF16).astype(F32)
            lo = (madd - hi).astype(BF16).astype(F32)
            qt = qt_all[cols, :] * (LOG2E * HEAD_DIM ** -0.5)
            qa_ref[b, h] = jnp.concatenate([qt, hi, lo, pad], axis=0).astype(BF16)
            ka_ref[b, :, h * width:h * width + HEAD_DIM] = k[:, cols].astype(BF16)
            ka_ref[b, :, h * width + HEAD_DIM:(h + 1) * width] = onehot
        kmean = jnp.mean(k, axis=0, keepdims=True)
        for h in range(HEADS):
            kmean_scr[b, h, pl.ds(j, 1), :] = jnp.where(col_head == h, kmean, 0.0)


def _moba_proj(h, gain, w_in, rel_table):
    bsz, seq, _ = h.shape
    nb = seq // MOBA_BLOCK
    assert 2 * nb <= HEAD_DIM
    return pl.pallas_call(
        _moba_proj_kernel,
        grid=(nb,),
        in_specs=[
            pl.BlockSpec((bsz, MOBA_BLOCK, D_MODEL), lambda j: (0, j, 0)),
            _resident((1, D_MODEL)),
            _resident((D_MODEL, 3 * D_MODEL)),
            _resident(rel_table.shape),
        ],
        out_specs=[
            pl.BlockSpec((bsz, HEADS, 2 * HEAD_DIM, MOBA_BLOCK), lambda j: (0, 0, 0, j)),
            pl.BlockSpec((bsz, MOBA_BLOCK, 2 * D_MODEL), lambda j: (0, j, 0)),
            pl.BlockSpec((bsz, HEADS, 1, HEAD_DIM, MOBA_BLOCK), lambda j: (0, 0, j, 0, 0)),
        ],
        out_shape=[
            jax.ShapeDtypeStruct((bsz, HEADS, 2 * HEAD_DIM, seq), BF16),
            jax.ShapeDtypeStruct((bsz, seq, 2 * D_MODEL), BF16),
            jax.ShapeDtypeStruct((bsz, HEADS, nb, HEAD_DIM, MOBA_BLOCK), BF16),
        ],
        scratch_shapes=[pltpu.VMEM((bsz, HEADS, nb, D_MODEL), F32)],
        compiler_params=pltpu.CompilerParams(
            dimension_semantics=("arbitrary",), vmem_limit_bytes=VMEM_LIMIT),
        name="moba_proj",
    )(h, gain.reshape(1, D_MODEL), w_in.astype(BF16), rel_table)


def _attn_tile(qa_ref, ka_ref, vt_ref, bias_ref, out_ref, b, j, n_groups):
    qa = qa_ref[b, 0]
    fold = (MOBA_BLOCK // SUBLANES, SUBLANES, MOBA_BLOCK)
    first_near = (n_groups - 2) * ATTN_GROUP

    def qk_group(g):
        tiles = []
        for n in range(g * ATTN_GROUP, (g + 1) * ATTN_GROUP):
            s = _dot(ka_ref[b, n * MOBA_BLOCK:(n + 1) * MOBA_BLOCK, :], qa)
            if n >= first_near:
                s = s + bias_ref[jnp.clip(j - n, 0, N_BIAS_TILES - 1), 0]
            tiles.append(s)
        return tiles

    m = jnp.full((1, MOBA_BLOCK), NEG, F32)
    lrun = jnp.zeros((SUBLANES, MOBA_BLOCK), F32)
    acc = jnp.zeros((HEAD_DIM, MOBA_BLOCK), F32)
    tiles = qk_group(0)
    yield
    for g in range(n_groups):
        nxt = qk_group(g + 1) if g + 1 < n_groups else None
        yield
        gmax = functools.reduce(jnp.maximum, [jnp.max(s.reshape(fold), axis=0) for s in tiles])
        m_new = jnp.maximum(m, jnp.max(gmax, axis=0, keepdims=True))
        alpha = jnp.exp2(m - m_new)
        lsum = jnp.zeros((SUBLANES, MOBA_BLOCK), F32)
        pv = jnp.zeros((HEAD_DIM, MOBA_BLOCK), F32)
        for u, s in enumerate(tiles):
            p = jnp.exp2(s - m_new)
            lsum = lsum + jnp.sum(p.reshape(fold), axis=0)
            pv = pv + _dot(vt_ref[b, 0, g * ATTN_GROUP + u], p.astype(BF16))
        lrun = alpha * lrun + lsum
        acc = alpha * acc + pv
        m = m_new
        tiles = nxt
        yield
    l = jnp.sum(lrun, axis=0, keepdims=True)
    out_ref[b] = (acc / l).T.astype(out_ref.dtype)


def _moba_attn_kernel(qa_ref, ka_ref, vt_ref, bias_ref, out_ref):
    j = pl.program_id(1)
    bsz = qa_ref.shape[0]
    nb = vt_ref.shape[2]
    for n_groups in range(1, nb // ATTN_GROUP + 1):
        @pl.when(j // ATTN_GROUP == n_groups - 1)
        def _():
            _run_staggered([_attn_tile(qa_ref, ka_ref, vt_ref, bias_ref, out_ref, b, j, n_groups)
                            for b in range(bsz)])


def _moba_attn(qa, ka, vt, bias):
    bsz, seq, _ = ka.shape
    nb = seq // MOBA_BLOCK
    assert nb % ATTN_GROUP == 0
    width = 2 * HEAD_DIM
    return pl.pallas_call(
        _moba_attn_kernel,
        grid=(HEADS, nb),
        in_specs=[
            pl.BlockSpec((bsz, 1, width, MOBA_BLOCK), lambda h, j: (0, h, 0, j)),
            pl.BlockSpec((bsz, seq, width), lambda h, j: (0, 0, h)),
            pl.BlockSpec((bsz, 1, nb, HEAD_DIM, MOBA_BLOCK), lambda h, j: (0, h, 0, 0, 0)),
            pl.BlockSpec((N_BIAS_TILES, 1, MOBA_BLOCK, MOBA_BLOCK), lambda h, j: (0, h, 0, 0)),
        ],
        out_specs=pl.BlockSpec((bsz, MOBA_BLOCK, HEAD_DIM), lambda h, j: (0, j, h)),
        out_shape=jax.ShapeDtypeStruct((bsz, seq, D_MODEL), BF16),
        compiler_params=pltpu.CompilerParams(
            dimension_semantics=("arbitrary", "arbitrary"), vmem_limit_bytes=VMEM_LIMIT),
        name="moba_attn",
    )(qa, ka, vt, bias)


def kernel(x, norm_mix, norm_ffn, hgrn_w_in, hgrn_lb_logits, hgrn_out_norm, hgrn_w_out,
           moba_w_in, moba_w_out, rel_bias_table, ffn_w13, ffn_w2, final_norm):
    depth = norm_mix.shape[0]
    n_mixers = 2
    bias = _bias_tiles(rel_bias_table)
    h = x
    for layer in range(depth):
        idx = layer // n_mixers
        if layer % n_mixers == 0:
            a = _hgrn_mixer(h, norm_mix[layer], hgrn_w_in[idx], hgrn_lb_logits, hgrn_out_norm[idx], idx)
            w_out = hgrn_w_out[idx]
        else:
            qa, ka, vt = _moba_proj(h, norm_mix[layer], moba_w_in[idx], rel_bias_table)
            a = _moba_attn(qa, ka, vt, bias)
            w_out = moba_w_out[idx]
        h = _outproj_ffn(h, a, w_out, norm_ffn[layer], ffn_w13[layer], ffn_w2[layer], final_norm,
                         final_norm=(layer == depth - 1))
    return h
```

```python
import functools
import math

import numpy as np
import jax
import jax.numpy as jnp
from jax import lax
from jax.experimental import pallas as pl
from jax.experimental.pallas import tpu as pltpu

D_MODEL = 1024
HEADS = 8
HEAD_DIM = D_MODEL // HEADS
MOBA_BLOCK = 256
MOBA_TOPK = 3
REL_BUCKETS = 32
REL_MAX_DISTANCE = 1024
RMS_EPS = 1e-6
HGRN_CHUNK = 128
FFN_ROWS = 512
ATTN_GROUP = 4
NEG = -1e30
N_BIAS_TILES = 6
VMEM_LIMIT = 56 * 1024 * 1024
LOG2E = math.log2(math.e)
SUBLANES = 8

BF16 = jnp.bfloat16
F32 = jnp.float32


def _resident(shape):
    zeros = (0,) * len(shape)
    return pl.BlockSpec(shape, lambda *_: zeros, pipeline_mode=pl.Buffered(1))


def _layer(stacked, idx):
    zeros = (0,) * (stacked.ndim - 1)
    return pl.BlockSpec((None,) + stacked.shape[1:], lambda *_: (idx,) + zeros, pipeline_mode=pl.Buffered(1))


def _sigmoid(x):
    return 1.0 / (1.0 + jnp.exp(-x))


def _rms(x, gain):
    return x * lax.rsqrt(jnp.mean(x * x, axis=-1, keepdims=True) + RMS_EPS) * gain


def _dot(a, b):
    return jnp.dot(a, b, preferred_element_type=F32)


def _run_staggered(sequences):
    live = list(enumerate(sequences))
    step = 0
    while live:
        for i, seq in list(live):
            if step >= i and next(seq, StopIteration) is StopIteration:
                live.remove((i, seq))
        step += 1


def _hgrn_tables(c):
    ms = [c >> (i + 1) for i in range(int(math.log2(c)))]
    r = np.arange(c)
    level = np.full((c, c), -1, np.int32)
    gsel = np.zeros((len(ms), c, c), np.float32)
    sgn = np.zeros((len(ms), c, HEAD_DIM), np.float32)
    for i, m in enumerate(ms):
        upper = (r % (2 * m)) >= m
        same = (r[:, None] // (2 * m)) == (r[None, :] // (2 * m))
        level[same & upper[:, None] & ~upper[None, :]] = i
        mid = (r // (2 * m)) * 2 * m + m - 1
        gsel[i, r, mid] = 1.0
        sgn[i] = np.where(upper, 1.0, -1.0)[:, None]
    level[r, r] = len(ms)
    tri = (r[:, None] >= r[None, :]).astype(np.float32)
    fine = [i for i, m in enumerate(ms) if 2 * m < SUBLANES]
    return (jnp.asarray(tri, BF16), jnp.asarray(gsel[fine].reshape(len(fine) * c, c), BF16), jnp.asarray(sgn, F32),
            jnp.asarray(level, jnp.int32), ms)


def _hgrn_kernel(x_ref, gain_ref, w_ref, lbl_ref, onorm_ref, tri_ref, gsel_ref, sgn_ref, lvl_ref,
                 a_ref, st_ref, *, layer_idx, ms):
    bsz, c, d = x_ref.shape
    n_levels = len(ms)

    @pl.when(pl.program_id(0) == 0)
    def _():
        st_ref[...] = jnp.zeros_like(st_ref)

    logits = [lbl_ref[r:r + 1, :] for r in range(lbl_ref.shape[0])]
    top = functools.reduce(jnp.maximum, logits)
    ex = [jnp.exp(t - top) for t in logits]
    lb = sum(ex[:layer_idx + 1]) / sum(ex)

    xn = _rms(x_ref[...].reshape(bsz * c, d), gain_ref[...]).astype(BF16)
    p_all = _dot(xn, w_ref[...])

    def chunk(b):
        p = p_all[b * c:(b + 1) * c]
        qz = p[:, :d]
        q = qz * _sigmoid(qz)
        f = lb + (1.0 - lb) * _sigmoid(p[:, d:2 * d])
        k = 1.0 - f
        g = jnp.log(f) * LOG2E
        inp_b = p[:, 2 * d:3 * d].astype(BF16)
        og = _sigmoid(p[:, 3 * d:])
        g0 = g.astype(BF16)
        r1 = g - g0.astype(F32)
        g1 = r1.astype(BF16)
        g2 = (r1 - g1.astype(F32)).astype(BF16)
        yield
        tri = tri_ref[...]
        cum = _dot(tri, g0) + _dot(tri, g1) + _dot(tri, g2)
        fine_refs = _dot(gsel_ref[...], cum.astype(BF16))
        yield
        qe, ke = [], []
        for i, m in enumerate(ms):
            if 2 * m >= SUBLANES:
                blocks = cum.reshape(c // (2 * m), 2 * m, d)
                ref = jnp.broadcast_to(blocks[:, m - 1:m, :], blocks.shape).reshape(c, d)
            else:
                fi = i - (n_levels - gsel_ref.shape[0] // c)
                ref = fine_refs[fi * c:(fi + 1) * c]
            e = jnp.exp2((cum - ref) * jnp.tile(sgn_ref[i], (1, HEADS)))
            qe.append((q * e).astype(BF16))
            ke.append((k * e).T.astype(BF16))
        qe.append(q.astype(BF16))
        ke.append(k.T.astype(BF16))
        last = cum[c - 1:c, :]
        qs = (q * jnp.exp2(cum)).astype(BF16)
        kd = (k * jnp.exp2(last - cum)).T.astype(BF16)
        decay = jnp.broadcast_to(jnp.exp2(last), (SUBLANES, d)).T[:, 0:1]
        yield
        lvl = lvl_ref[...]
        outs = []
        for h in range(HEADS):
            cols = slice(h * HEAD_DIM, (h + 1) * HEAD_DIM)
            scores = jnp.zeros((c, c), F32)
            for i in range(n_levels + 1):
                scores = jnp.where(lvl == i, _dot(qe[i][:, cols], ke[i][cols, :]), scores)
            st = st_ref[b, h]
            o = _dot(scores.astype(BF16), inp_b[:, cols]) + _dot(qs[:, cols], st.astype(BF16))
            st_ref[b, h] = st * decay[cols, :] + _dot(kd[cols, :], inp_b[:, cols])
            outs.append(o)
        yield
        ss = sum(jnp.sum(o * o, axis=-1, keepdims=True) for o in outs)
        inv = lax.rsqrt(ss / D_MODEL + RMS_EPS)
        for h in range(HEADS):
            cols = slice(h * HEAD_DIM, (h + 1) * HEAD_DIM)
            a_ref[b, :, cols] = (outs[h] * inv * onorm_ref[:, cols] * og[:, cols]).astype(a_ref.dtype)

    _run_staggered([chunk(b) for b in range(bsz)])


def _hgrn_mixer(x, gains, layer, w_in, lb_logits, out_norms, layer_idx):
    bsz, seq, _ = x.shape
    c = HGRN_CHUNK
    tri, gsel, sgn, lvl, ms = _hgrn_tables(c)
    kern = functools.partial(_hgrn_kernel, layer_idx=layer_idx, ms=tuple(ms))
    return pl.pallas_call(
        kern,
        grid=(seq // c,),
        in_specs=[
            pl.BlockSpec((bsz, c, D_MODEL), lambda i: (0, i, 0)),
            _layer(gains, layer),
            _layer(w_in, layer_idx),
            _resident(lb_logits.shape),
            _layer(out_norms, layer_idx),
            _resident(tri.shape), _resident(gsel.shape), _resident(sgn.shape), _resident(lvl.shape),
        ],
        out_specs=pl.BlockSpec((bsz, c, D_MODEL), lambda i: (0, i, 0)),
        out_shape=jax.ShapeDtypeStruct((bsz, seq, D_MODEL), BF16),
        scratch_shapes=[pltpu.VMEM((bsz, HEADS, HEAD_DIM, HEAD_DIM), F32)],
        compiler_params=pltpu.CompilerParams(
            dimension_semantics=("arbitrary",), vmem_limit_bytes=VMEM_LIMIT),
        name="hgrn_mixer",
    )(x, gains, w_in, lb_logits, out_norms, tri, gsel, sgn, lvl)


def _ffn_kernel(res_ref, a_ref, wo_ref, gain_ref, w13_ref, w2_ref, fin_ref, out_ref, *, final_norm):
    d_ff = w2_ref.shape[0]
    h = res_ref[...] + _dot(a_ref[...], wo_ref[...])
    xn = _rms(h, gain_ref[...]).astype(BF16)
    gu = _dot(xn, w13_ref[...])
    g = gu[:, :d_ff]
    u = gu[:, d_ff:]
    act = (g * _sigmoid(g) * u).astype(BF16)
    h = h + _dot(act, w2_ref[...])
    if final_norm:
        h = _rms(h, fin_ref[...])
    out_ref[...] = h


def _outproj_ffn(res, a, w_out, idx, gains, w13, w2, layer, final_gain, final_norm):
    bsz, seq, _ = res.shape
    rows = bsz * seq
    kern = functools.partial(_ffn_kernel, final_norm=final_norm)
    row_spec = pl.BlockSpec((FFN_ROWS, D_MODEL), lambda i: (i, 0))
    out = pl.pallas_call(
        kern,
        grid=(rows // FFN_ROWS,),
        in_specs=[
            row_spec, row_spec,
            _layer(w_out, idx),
            _layer(gains, layer),
            _layer(w13, layer),
            _layer(w2, layer),
            _resident((1, D_MODEL)),
        ],
        out_specs=row_spec,
        out_shape=jax.ShapeDtypeStruct((rows, D_MODEL), F32),
        compiler_params=pltpu.CompilerParams(
            dimension_semantics=("arbitrary",), vmem_limit_bytes=VMEM_LIMIT),
        name="outproj_ffn_final" if final_norm else "outproj_ffn",
    )(res.reshape(rows, D_MODEL), a.reshape(rows, D_MODEL), w_out, gains, w13, w2, final_gain.reshape(1, D_MODEL))
    return out.reshape(bsz, seq, D_MODEL)


def _t5_thresholds():
    max_exact = REL_BUCKETS // 2
    n_log = REL_BUCKETS - max_exact
    ratio = REL_MAX_DISTANCE / max_exact
    return [int(math.ceil(max_exact * ratio ** (k / n_log) - 1e-9)) for k in range(1, n_log)]


def _t5_bucket(dist):
    max_exact = REL_BUCKETS // 2
    return dist if dist < max_exact else max_exact + sum(dist >= t for t in _t5_thresholds())


def _bias_kernel(tab_ref, out_ref):
    dl = pl.program_id(0)
    tk = lax.broadcasted_iota(jnp.int32, (MOBA_BLOCK, MOBA_BLOCK), 0)
    tq = lax.broadcasted_iota(jnp.int32, (MOBA_BLOCK, MOBA_BLOCK), 1)
    max_exact = REL_BUCKETS // 2
    for tile in range(N_BIAS_TILES - 1):
        @pl.when(dl == tile)
        def _():
            dist = tile * MOBA_BLOCK + tq - tk
            d_lo = max(tile * MOBA_BLOCK - (MOBA_BLOCK - 1), 0)
            d_hi = tile * MOBA_BLOCK + MOBA_BLOCK - 1
            b_lo, b_hi = _t5_bucket(d_lo), _t5_bucket(d_hi)
            large = jnp.full(dist.shape, max_exact, jnp.int32)
            for t in _t5_thresholds():
                if t <= d_hi:
                    large = large + (dist >= t).astype(jnp.int32)
            bucket = jnp.where(dist < max_exact, dist, large) if d_lo < max_exact else large
            for h in range(HEADS):
                bias = jnp.full(dist.shape, tab_ref[b_lo, h] * LOG2E, F32)
                for bkt in range(b_lo + 1, b_hi + 1):
                    bias = jnp.where(bucket == bkt, tab_ref[bkt, h] * LOG2E, bias)
                if tile == 0:
                    bias = jnp.where(dist < 0, NEG, bias)
                out_ref[0, h] = bias

    @pl.when(dl == N_BIAS_TILES - 1)
    def _():
        out_ref[...] = jnp.zeros_like(out_ref)


def _bias_tiles(rel_table):
    return pl.pallas_call(
        _bias_kernel,
        grid=(N_BIAS_TILES,),
        in_specs=[pl.BlockSpec(memory_space=pltpu.SMEM)],
        out_specs=pl.BlockSpec((1, HEADS, MOBA_BLOCK, MOBA_BLOCK), lambda i: (i, 0, 0, 0)),
        out_shape=jax.ShapeDtypeStruct((N_BIAS_TILES, HEADS, MOBA_BLOCK, MOBA_BLOCK), F32),
        compiler_params=pltpu.CompilerParams(dimension_semantics=("arbitrary",)),
        name="moba_bias_tiles",
    )(rel_table)


def _moba_proj_kernel(x_ref, gain_ref, w_ref, tab_ref, qa_ref, ka_ref, vt_ref, kmean_scr):
    j = pl.program_id(0)
    bsz, nb = kmean_scr.shape[0], kmean_scr.shape[2]
    width = 2 * HEAD_DIM
    col_head = lax.broadcasted_iota(jnp.int32, (1, D_MODEL), 1) // HEAD_DIM

    @pl.when(j == 0)
    def _():
        kmean_scr[...] = jnp.zeros_like(kmean_scr)

    xn = _rms(x_ref[...].reshape(bsz * MOBA_BLOCK, D_MODEL), gain_ref[...]).astype(BF16)
    qkv = _dot(xn, w_ref[...])

    lane = lax.broadcasted_iota(jnp.int32, (MOBA_BLOCK, HEAD_DIM), 1)
    onehot = ((lane == j) | (lane == j + nb)).astype(BF16)
    blk = lax.broadcasted_iota(jnp.int32, (HEADS, nb, MOBA_BLOCK), 1)
    blk_f = blk.astype(F32)
    far_bias = jnp.concatenate(
        [jnp.broadcast_to(tab_ref[REL_BUCKETS - 1:REL_BUCKETS, h:h + 1] * LOG2E, (1, nb, MOBA_BLOCK)) for h in range(HEADS)],
        axis=0)
    far_bias = jnp.where(j - blk >= N_BIAS_TILES - 1, far_bias, 0.0)
    pad = jnp.zeros((HEAD_DIM - 2 * nb, MOBA_BLOCK), F32)
    for b in range(bsz):
        rows = slice(b * MOBA_BLOCK, (b + 1) * MOBA_BLOCK)
        q = qkv[rows, :D_MODEL]
        k = qkv[rows, D_MODEL:2 * D_MODEL]
        v = qkv[rows, 2 * D_MODEL:]
        vt_ref[b, :, 0] = v.T.reshape(HEADS, HEAD_DIM, MOBA_BLOCK).astype(BF16)
        qt_all = q.T
        km = kmean_scr[b].reshape(HEADS * nb, D_MODEL)
        km_hi = km.astype(BF16)
        km_lo = (km - km_hi.astype(F32)).astype(BF16)
        qt_hi = qt_all.astype(BF16)
        qt_lo = (qt_all - qt_hi.astype(F32)).astype(BF16)
        gate = (_dot(km_hi, qt_hi) + _dot(km_hi, qt_lo) + _dot(km_lo, qt_hi)).reshape(HEADS, nb, MOBA_BLOCK)
        gate = jnp.where(blk < j, gate, -jnp.inf)
        madd = jnp.full(gate.shape, NEG, F32)
        for _ in range(MOBA_TOPK):
            top = jnp.max(gate, axis=1, keepdims=True)
            idx = jnp.min(jnp.where(gate == top, blk_f, float(nb)), axis=1, keepdims=True)
            hit = blk_f == idx
            madd = jnp.where(hit & (top > -jnp.inf), far_bias, madd)
            gate = jnp.where(hit, -jnp.inf, gate)
        madd = jnp.where(blk == j, 0.0, madd)
        hi = madd.astype(BF16).astype(F32)
        lo = (madd - hi).astype(BF16).astype(F32)
        for h in range(HEADS):
            cols = slice(h * HEAD_DIM, (h + 1) * HEAD_DIM)
            qt = qt_all[cols, :] * (LOG2E * HEAD_DIM ** -0.5)
            qa_ref[b, h] = jnp.concatenate([qt, hi[h], lo[h], pad], axis=0).astype(BF16)
            ka_ref[b, :, h * width:h * width + HEAD_DIM] = k[:, cols].astype(BF16)
            ka_ref[b, :, h * width + HEAD_DIM:(h + 1) * width] = onehot
        kmean = jnp.mean(k, axis=0, keepdims=True)
        for h in range(HEADS):
            kmean_scr[b, h, pl.ds(j, 1), :] = jnp.where(col_head == h, kmean, 0.0)


def _moba_proj(h, gains, layer, w_in, idx, rel_table):
    bsz, seq, _ = h.shape
    nb = seq // MOBA_BLOCK
    assert 2 * nb <= HEAD_DIM
    return pl.pallas_call(
        _moba_proj_kernel,
        grid=(nb,),
        in_specs=[
            pl.BlockSpec((bsz, MOBA_BLOCK, D_MODEL), lambda j: (0, j, 0)),
            _layer(gains, layer),
            _layer(w_in, idx),
            _resident(rel_table.shape),
        ],
        out_specs=[
            pl.BlockSpec((bsz, HEADS, 2 * HEAD_DIM, MOBA_BLOCK), lambda j: (0, 0, 0, j)),
            pl.BlockSpec((bsz, MOBA_BLOCK, 2 * D_MODEL), lambda j: (0, j, 0)),
            pl.BlockSpec((bsz, HEADS, 1, HEAD_DIM, MOBA_BLOCK), lambda j: (0, 0, j, 0, 0)),
        ],
        out_shape=[
            jax.ShapeDtypeStruct((bsz, HEADS, 2 * HEAD_DIM, seq), BF16),
            jax.ShapeDtypeStruct((bsz, seq, 2 * D_MODEL), BF16),
            jax.ShapeDtypeStruct((bsz, HEADS, nb, HEAD_DIM, MOBA_BLOCK), BF16),
        ],
        scratch_shapes=[pltpu.VMEM((bsz, HEADS, nb, D_MODEL), F32)],
        compiler_params=pltpu.CompilerParams(
            dimension_semantics=("arbitrary",), vmem_limit_bytes=VMEM_LIMIT),
        name="moba_proj",
    )(h, gains, w_in, rel_table)


def _attn_tile(qa_ref, ka_ref, vt_ref, bias_ref, out_ref, b, j, n_groups):
    qa = qa_ref[b, 0]
    fold = (MOBA_BLOCK // SUBLANES, SUBLANES, MOBA_BLOCK)
    first_near = (n_groups - 2) * ATTN_GROUP

    def qk_group(g):
        tiles = []
        for n in range(g * ATTN_GROUP, (g + 1) * ATTN_GROUP):
            s = _dot(ka_ref[b, n * MOBA_BLOCK:(n + 1) * MOBA_BLOCK, :], qa)
            if n >= first_near:
                s = s + bias_ref[jnp.clip(j - n, 0, N_BIAS_TILES - 1), 0]
            tiles.append(s)
        return tiles

    m = jnp.full((1, MOBA_BLOCK), NEG, F32)
    lrun = jnp.zeros((SUBLANES, MOBA_BLOCK), F32)
    acc = jnp.zeros((HEAD_DIM, MOBA_BLOCK), F32)
    tiles = qk_group(0)
    yield
    for g in range(n_groups):
        nxt = qk_group(g + 1) if g + 1 < n_groups else None
        yield
        gmax = functools.reduce(jnp.maximum, [jnp.max(s.reshape(fold), axis=0) for s in tiles])
        m_new = jnp.maximum(m, jnp.max(gmax, axis=0, keepdims=True))
        alpha = jnp.exp2(m - m_new)
        lsum = jnp.zeros((SUBLANES, MOBA_BLOCK), F32)
        pv = jnp.zeros((HEAD_DIM, MOBA_BLOCK), F32)
        for u, s in enumerate(tiles):
            p = jnp.exp2(s - m_new)
            lsum = lsum + jnp.sum(p.reshape(fold), axis=0)
            pv = pv + _dot(vt_ref[b, 0, g * ATTN_GROUP + u], p.astype(BF16))
        lrun = alpha * lrun + lsum
        acc = alpha * acc + pv
        m = m_new
        tiles = nxt
        yield
    l = jnp.sum(lrun, axis=0, keepdims=True)
    out_ref[b] = (acc / l).T.astype(out_ref.dtype)


def _moba_attn_kernel(qa_ref, ka_ref, vt_ref, bias_ref, out_ref):
    j = pl.program_id(1)
    bsz = qa_ref.shape[0]
    nb = vt_ref.shape[2]
    for n_groups in range(1, nb // ATTN_GROUP + 1):
        @pl.when(j // ATTN_GROUP == n_groups - 1)
        def _():
            _run_staggered([_attn_tile(qa_ref, ka_ref, vt_ref, bias_ref, out_ref, b, j, n_groups)
                            for b in range(bsz)])


def _moba_attn(qa, ka, vt, bias):
    bsz, seq, _ = ka.shape
    nb = seq // MOBA_BLOCK
    assert nb % ATTN_GROUP == 0
    width = 2 * HEAD_DIM
    return pl.pallas_call(
        _moba_attn_kernel,
        grid=(HEADS, nb),
        in_specs=[
            pl.BlockSpec((bsz, 1, width, MOBA_BLOCK), lambda h, j: (0, h, 0, j)),
            pl.BlockSpec((bsz, seq, width), lambda h, j: (0, 0, h)),
            pl.BlockSpec((bsz, 1, nb, HEAD_DIM, MOBA_BLOCK), lambda h, j: (0, h, 0, 0, 0)),
            pl.BlockSpec((N_BIAS_TILES, 1, MOBA_BLOCK, MOBA_BLOCK), lambda h, j: (0, h, 0, 0)),
        ],
        out_specs=pl.BlockSpec((bsz, MOBA_BLOCK, HEAD_DIM), lambda h, j: (0, j, h)),
        out_shape=jax.ShapeDtypeStruct((bsz, seq, D_MODEL), BF16),
        compiler_params=pltpu.CompilerParams(
            dimension_semantics=("arbitrary", "arbitrary"), vmem_limit_bytes=VMEM_LIMIT),
        name="moba_attn",
    )(qa, ka, vt, bias)


def kernel(x, norm_mix, norm_ffn, hgrn_w_in, hgrn_lb_logits, hgrn_out_norm, hgrn_w_out,
           moba_w_in, moba_w_out, rel_bias_table, ffn_w13, ffn_w2, final_norm):
    depth = norm_mix.shape[0]
    n_mixers = 2
    hgrn_w_in, hgrn_w_out, moba_w_in, moba_w_out, ffn_w13, ffn_w2 = (
        w.astype(BF16) for w in (hgrn_w_in, hgrn_w_out, moba_w_in, moba_w_out, ffn_w13, ffn_w2))
    norm_mix, norm_ffn, hgrn_out_norm = (g[:, None, :] for g in (norm_mix, norm_ffn, hgrn_out_norm))
    bias = _bias_tiles(rel_bias_table)
    h = x
    for layer in range(depth):
        idx = layer // n_mixers
        if layer % n_mixers == 0:
            a = _hgrn_mixer(h, norm_mix, layer, hgrn_w_in, hgrn_lb_logits, hgrn_out_norm, idx)
            w_out = hgrn_w_out
        else:
            qa, ka, vt = _moba_proj(h, norm_mix, layer, moba_w_in, idx, rel_bias_table)
            a = _moba_attn(qa, ka, vt, bias)
            w_out = moba_w_out
        h = _outproj_ffn(h, a, w_out, idx, norm_ffn, ffn_w13, ffn_w2, layer, final_norm,
                         final_norm=(layer == depth - 1))
    return h
```

```python
import functools
import math

import numpy as np
import jax
import jax.numpy as jnp
from jax import lax
from jax.experimental import pallas as pl
from jax.experimental.pallas import tpu as pltpu

D_MODEL = 1024
HEADS = 8
HEAD_DIM = D_MODEL // HEADS
MOBA_BLOCK = 256
MOBA_TOPK = 3
REL_BUCKETS = 32
REL_MAX_DISTANCE = 1024
RMS_EPS = 1e-6
HGRN_CHUNK = 128
FFN_ROWS = 512
ATTN_GROUP = 4
NEG = -1e30
N_BIAS_TILES = 6
VMEM_LIMIT = 56 * 1024 * 1024
LOG2E = math.log2(math.e)
SUBLANES = 8

BF16 = jnp.bfloat16
F32 = jnp.float32


def _resident(shape):
    zeros = (0,) * len(shape)
    return pl.BlockSpec(shape, lambda *_: zeros, pipeline_mode=pl.Buffered(1))


def _layer(stacked, idx):
    zeros = (0,) * (stacked.ndim - 1)
    return pl.BlockSpec((None,) + stacked.shape[1:], lambda *_: (idx,) + zeros, pipeline_mode=pl.Buffered(1))


def _sigmoid(x):
    return 1.0 / (1.0 + jnp.exp(-x))


def _rms(x, gain):
    return x * lax.rsqrt(jnp.mean(x * x, axis=-1, keepdims=True) + RMS_EPS) * gain


def _dot(a, b):
    return jnp.dot(a, b, preferred_element_type=F32)


def _run_staggered(sequences):
    live = list(enumerate(sequences))
    step = 0
    while live:
        for i, seq in list(live):
            if step >= i and next(seq, StopIteration) is StopIteration:
                live.remove((i, seq))
        step += 1


def _hgrn_tables(c):
    ms = [c >> (i + 1) for i in range(int(math.log2(c)))]
    r = np.arange(c)
    level = np.full((c, c), -1, np.int32)
    gsel = np.zeros((len(ms), c, c), np.float32)
    sgn = np.zeros((len(ms), c, HEAD_DIM), np.float32)
    for i, m in enumerate(ms):
        upper = (r % (2 * m)) >= m
        same = (r[:, None] // (2 * m)) == (r[None, :] // (2 * m))
        level[same & upper[:, None] & ~upper[None, :]] = i
        mid = (r // (2 * m)) * 2 * m + m - 1
        gsel[i, r, mid] = 1.0
        sgn[i] = np.where(upper, 1.0, -1.0)[:, None]
    level[r, r] = len(ms)
    tri = (r[:, None] >= r[None, :]).astype(np.float32)
    fine = [i for i, m in enumerate(ms) if 2 * m < SUBLANES]
    return (jnp.asarray(tri, BF16), jnp.asarray(gsel[fine].reshape(len(fine) * c, c), BF16), jnp.asarray(sgn, F32),
            jnp.asarray(level, jnp.int32), ms)


def _hgrn_kernel(x_ref, gain_ref, w_ref, lbl_ref, onorm_ref, tri_ref, gsel_ref, sgn_ref, lvl_ref,
                 a_ref, st_ref, *, layer_idx, ms):
    bsz, c, d = x_ref.shape
    n_levels = len(ms)

    @pl.when(pl.program_id(0) == 0)
    def _():
        st_ref[...] = jnp.zeros_like(st_ref)

    logits = [lbl_ref[r:r + 1, :] for r in range(lbl_ref.shape[0])]
    top = functools.reduce(jnp.maximum, logits)
    ex = [jnp.exp(t - top) for t in logits]
    lb = sum(ex[:layer_idx + 1]) / sum(ex)

    xn = _rms(x_ref[...].reshape(bsz * c, d), gain_ref[...]).astype(BF16)
    p_all = _dot(xn, w_ref[...])

    def chunk(b):
        p = p_all[b * c:(b + 1) * c]
        qz = p[:, :d]
        q = qz * _sigmoid(qz)
        f = lb + (1.0 - lb) * _sigmoid(p[:, d:2 * d])
        k = 1.0 - f
        g = jnp.log(f) * LOG2E
        inp_b = p[:, 2 * d:3 * d].astype(BF16)
        og = _sigmoid(p[:, 3 * d:])
        g0 = g.astype(BF16)
        r1 = g - g0.astype(F32)
        g1 = r1.astype(BF16)
        g2 = (r1 - g1.astype(F32)).astype(BF16)
        yield
        tri = tri_ref[...]
        cum = _dot(tri, g0) + _dot(tri, g1) + _dot(tri, g2)
        fine_refs = _dot(gsel_ref[...], cum.astype(BF16))
        yield
        qe, ke = [], []
        for i, m in enumerate(ms):
            if 2 * m >= SUBLANES:
                blocks = cum.reshape(c // (2 * m), 2 * m, d)
                ref = jnp.broadcast_to(blocks[:, m - 1:m, :], blocks.shape).reshape(c, d)
            else:
                fi = i - (n_levels - gsel_ref.shape[0] // c)
                ref = fine_refs[fi * c:(fi + 1) * c]
            e = jnp.exp2((cum - ref) * jnp.tile(sgn_ref[i], (1, HEADS)))
            qe.append((q * e).astype(BF16))
            ke.append((k * e).T.astype(BF16))
        qe.append(q.astype(BF16))
        ke.append(k.T.astype(BF16))
        last = cum[c - 1:c, :]
        qs = (q * jnp.exp2(cum)).astype(BF16)
        kd = (k * jnp.exp2(last - cum)).T.astype(BF16)
        decay = jnp.broadcast_to(jnp.exp2(last), (SUBLANES, d)).T[:, 0:1]
        yield
        lvl = lvl_ref[...]
        outs = []
        for h in range(HEADS):
            cols = slice(h * HEAD_DIM, (h + 1) * HEAD_DIM)
            scores = jnp.zeros((c, c), F32)
            for i in range(n_levels + 1):
                scores = jnp.where(lvl == i, _dot(qe[i][:, cols], ke[i][cols, :]), scores)
            st = st_ref[b, h]
            o = _dot(scores.astype(BF16), inp_b[:, cols]) + _dot(qs[:, cols], st.astype(BF16))
            st_ref[b, h] = st * decay[cols, :] + _dot(kd[cols, :], inp_b[:, cols])
            outs.append(o)
        yield
        ss = sum(jnp.sum(o * o, axis=-1, keepdims=True) for o in outs)
        inv = lax.rsqrt(ss / D_MODEL + RMS_EPS)
        for h in range(HEADS):
            cols = slice(h * HEAD_DIM, (h + 1) * HEAD_DIM)
            a_ref[b, :, cols] = (outs[h] * inv * onorm_ref[:, cols] * og[:, cols]).astype(a_ref.dtype)

    _run_staggered([chunk(b) for b in range(bsz)])


def _hgrn_mixer(x, gains, layer, w_in, lb_logits, out_norms, layer_idx):
    bsz, seq, _ = x.shape
    c = HGRN_CHUNK
    tri, gsel, sgn, lvl, ms = _hgrn_tables(c)
    kern = functools.partial(_hgrn_kernel, layer_idx=layer_idx, ms=tuple(ms))
    return pl.pallas_call(
        kern,
        grid=(seq // c,),
        in_specs=[
            pl.BlockSpec((bsz, c, D_MODEL), lambda i: (0, i, 0)),
            _layer(gains, layer),
            _layer(w_in, layer_idx),
            _resident(lb_logits.shape),
            _layer(out_norms, layer_idx),
            _resident(tri.shape), _resident(gsel.shape), _resident(sgn.shape), _resident(lvl.shape),
        ],
        out_specs=pl.BlockSpec((bsz, c, D_MODEL), lambda i: (0, i, 0)),
        out_shape=jax.ShapeDtypeStruct((bsz, seq, D_MODEL), BF16),
        scratch_shapes=[pltpu.VMEM((bsz, HEADS, HEAD_DIM, HEAD_DIM), F32)],
        compiler_params=pltpu.CompilerParams(
            dimension_semantics=("arbitrary",), vmem_limit_bytes=VMEM_LIMIT),
        name="hgrn_mixer",
    )(x, gains, w_in, lb_logits, out_norms, tri, gsel, sgn, lvl)


def _ffn_kernel(res_ref, a_ref, wo_ref, gain_ref, w13_ref, w2_ref, fin_ref, out_ref, *, final_norm):
    d_ff = w2_ref.shape[0]
    h = res_ref[...] + _dot(a_ref[...], wo_ref[...])
    xn = _rms(h, gain_ref[...]).astype(BF16)
    gu = _dot(xn, w13_ref[...])
    g = gu[:, :d_ff]
    u = gu[:, d_ff:]
    act = (g * _sigmoid(g) * u).astype(BF16)
    h = h + _dot(act, w2_ref[...])
    if final_norm:
        h = _rms(h, fin_ref[...])
    out_ref[...] = h


def _outproj_ffn(res, a, w_out, idx, gains, w13, w2, layer, final_gain, final_norm):
    bsz, seq, _ = res.shape
    rows = bsz * seq
    kern = functools.partial(_ffn_kernel, final_norm=final_norm)
    row_spec = pl.BlockSpec((FFN_ROWS, D_MODEL), lambda i: (i, 0))
    out = pl.pallas_call(
        kern,
        grid=(rows // FFN_ROWS,),
        in_specs=[
            row_spec, row_spec,
            _layer(w_out, idx),
            _layer(gains, layer),
            _layer(w13, layer),
            _layer(w2, layer),
            _resident((1, D_MODEL)),
        ],
        out_specs=row_spec,
        out_shape=jax.ShapeDtypeStruct((rows, D_MODEL), F32),
        compiler_params=pltpu.CompilerParams(
            dimension_semantics=("arbitrary",), vmem_limit_bytes=VMEM_LIMIT),
        name="outproj_ffn_final" if final_norm else "outproj_ffn",
    )(res.reshape(rows, D_MODEL), a.reshape(rows, D_MODEL), w_out, gains, w13, w2, final_gain.reshape(1, D_MODEL))
    return out.reshape(bsz, seq, D_MODEL)


def _t5_thresholds():
    max_exact = REL_BUCKETS // 2
    n_log = REL_BUCKETS - max_exact
    ratio = REL_MAX_DISTANCE / max_exact
    return [int(math.ceil(max_exact * ratio ** (k / n_log) - 1e-9)) for k in range(1, n_log)]


def _t5_bucket(dist):
    max_exact = REL_BUCKETS // 2
    return dist if dist < max_exact else max_exact + sum(dist >= t for t in _t5_thresholds())


def _bias_kernel(tab_ref, out_ref):
    dl = pl.program_id(0)
    tk = lax.broadcasted_iota(jnp.int32, (MOBA_BLOCK, MOBA_BLOCK), 0)
    tq = lax.broadcasted_iota(jnp.int32, (MOBA_BLOCK, MOBA_BLOCK), 1)
    max_exact = REL_BUCKETS // 2
    for tile in range(N_BIAS_TILES - 1):
        @pl.when(dl == tile)
        def _():
            dist = tile * MOBA_BLOCK + tq - tk
            d_lo = max(tile * MOBA_BLOCK - (MOBA_BLOCK - 1), 0)
            d_hi = tile * MOBA_BLOCK + MOBA_BLOCK - 1
            b_lo, b_hi = _t5_bucket(d_lo), _t5_bucket(d_hi)
            large = jnp.full(dist.shape, max_exact, jnp.int32)
            for t in _t5_thresholds():
                if t <= d_hi:
                    large = large + (dist >= t).astype(jnp.int32)
            bucket = jnp.where(dist < max_exact, dist, large) if d_lo < max_exact else large
            for h in range(HEADS):
                bias = jnp.full(dist.shape, tab_ref[b_lo, h] * LOG2E, F32)
                for bkt in range(b_lo + 1, b_hi + 1):
                    bias = jnp.where(bucket == bkt, tab_ref[bkt, h] * LOG2E, bias)
                if tile == 0:
                    bias = jnp.where(dist < 0, NEG, bias)
                out_ref[0, h] = bias

    @pl.when(dl == N_BIAS_TILES - 1)
    def _():
        out_ref[...] = jnp.zeros_like(out_ref)


def _bias_tiles(rel_table):
    return pl.pallas_call(
        _bias_kernel,
        grid=(N_BIAS_TILES,),
        in_specs=[pl.BlockSpec(memory_space=pltpu.SMEM)],
        out_specs=pl.BlockSpec((1, HEADS, MOBA_BLOCK, MOBA_BLOCK), lambda i: (i, 0, 0, 0)),
        out_shape=jax.ShapeDtypeStruct((N_BIAS_TILES, HEADS, MOBA_BLOCK, MOBA_BLOCK), F32),
        compiler_params=pltpu.CompilerParams(dimension_semantics=("arbitrary",)),
        name="moba_bias_tiles",
    )(rel_table)


def _moba_proj_kernel(x_ref, gain_ref, w_ref, tab_ref, qa_ref, ka_ref, vt_ref, kmean_scr):
    j = pl.program_id(0)
    bsz, nb = kmean_scr.shape[0], kmean_scr.shape[2]
    width = 2 * HEAD_DIM
    col_head = lax.broadcasted_iota(jnp.int32, (1, D_MODEL), 1) // HEAD_DIM

    @pl.when(j == 0)
    def _():
        kmean_scr[...] = jnp.zeros_like(kmean_scr)

    xn = _rms(x_ref[...].reshape(bsz * MOBA_BLOCK, D_MODEL), gain_ref[...]).astype(BF16)
    qkv = _dot(xn, w_ref[...])

    lane = lax.broadcasted_iota(jnp.int32, (MOBA_BLOCK, HEAD_DIM), 1)
    onehot = ((lane == j) | (lane == j + nb)).astype(BF16)
    blk = lax.broadcasted_iota(jnp.int32, (HEADS, nb, MOBA_BLOCK), 1)
    blk_f = blk.astype(F32)
    far_bias = jnp.concatenate(
        [jnp.broadcast_to(tab_ref[REL_BUCKETS - 1:REL_BUCKETS, h:h + 1] * LOG2E, (1, nb, MOBA_BLOCK)) for h in range(HEADS)],
        axis=0)
    far_bias = jnp.where(j - blk >= N_BIAS_TILES - 1, far_bias, 0.0)
    pad = jnp.zeros((HEAD_DIM - 2 * nb, MOBA_BLOCK), F32)
    for b in range(bsz):
        rows = slice(b * MOBA_BLOCK, (b + 1) * MOBA_BLOCK)
        q = qkv[rows, :D_MODEL]
        k = qkv[rows, D_MODEL:2 * D_MODEL]
        v = qkv[rows, 2 * D_MODEL:]
        vt_ref[b, :, 0] = v.T.reshape(HEADS, HEAD_DIM, MOBA_BLOCK).astype(BF16)
        qt_all = q.T
        km = kmean_scr[b].reshape(HEADS * nb, D_MODEL)
        km_hi = km.astype(BF16)
        km_lo = (km - km_hi.astype(F32)).astype(BF16)
        qt_hi = qt_all.astype(BF16)
        qt_lo = (qt_all - qt_hi.astype(F32)).astype(BF16)
        gate = (_dot(km_hi, qt_hi) + _dot(km_hi, qt_lo) + _dot(km_lo, qt_hi)).reshape(HEADS, nb, MOBA_BLOCK)
        gate = jnp.where(blk < j, gate, -jnp.inf)
        madd = jnp.full(gate.shape, NEG, F32)
        for _ in range(MOBA_TOPK):
            top = jnp.max(gate, axis=1, keepdims=True)
            idx = jnp.min(jnp.where(gate == top, blk_f, float(nb)), axis=1, keepdims=True)
            hit = blk_f == idx
            madd = jnp.where(hit & (top > -jnp.inf), far_bias, madd)
            gate = jnp.where(hit, -jnp.inf, gate)
        madd = jnp.where(blk == j, 0.0, madd)
        hi = madd.astype(BF16).astype(F32)
        lo = (madd - hi).astype(BF16).astype(F32)
        for h in range(HEADS):
            cols = slice(h * HEAD_DIM, (h + 1) * HEAD_DIM)
            qt = qt_all[cols, :] * (LOG2E * HEAD_DIM ** -0.5)
            qa_ref[b, h] = jnp.concatenate([qt, hi[h], lo[h], pad], axis=0).astype(BF16)
            ka_ref[b, :, h * width:h * width + HEAD_DIM] = k[:, cols].astype(BF16)
            ka_ref[b, :, h * width + HEAD_DIM:(h + 1) * width] = onehot
        kmean = jnp.mean(k, axis=0, keepdims=True)
        for h in range(HEADS):
            kmean_scr[b, h, pl.ds(j, 1), :] = jnp.where(col_head == h, kmean, 0.0)


def _moba_proj(h, gains, layer, w_in, idx, rel_table):
    bsz, seq, _ = h.shape
    nb = seq // MOBA_BLOCK
    assert 2 * nb <= HEAD_DIM
    return pl.pallas_call(
        _moba_proj_kernel,
        grid=(nb,),
        in_specs=[
            pl.BlockSpec((bsz, MOBA_BLOCK, D_MODEL), lambda j: (0, j, 0)),
            _layer(gains, layer),
            _layer(w_in, idx),
            _resident(rel_table.shape),
        ],
        out_specs=[
            pl.BlockSpec((bsz, HEADS, 2 * HEAD_DIM, MOBA_BLOCK), lambda j: (0, 0, 0, j)),
            pl.BlockSpec((bsz, MOBA_BLOCK, 2 * D_MODEL), lambda j: (0, j, 0)),
            pl.BlockSpec((bsz, HEADS, 1, HEAD_DIM, MOBA_BLOCK), lambda j: (0, 0, j, 0, 0)),
        ],
        out_shape=[
            jax.ShapeDtypeStruct((bsz, HEADS, 2 * HEAD_DIM, seq), BF16),
            jax.ShapeDtypeStruct((bsz, seq, 2 * D_MODEL), BF16),
            jax.ShapeDtypeStruct((bsz, HEADS, nb, HEAD_DIM, MOBA_BLOCK), BF16),
        ],
        scratch_shapes=[pltpu.VMEM((bsz, HEADS, nb, D_MODEL), F32)],
        compiler_params=pltpu.CompilerParams(
            dimension_semantics=("arbitrary",), vmem_limit_bytes=VMEM_LIMIT),
        name="moba_proj",
    )(h, gains, w_in, rel_table)


def _attn_tile(qa_ref, ka_ref, vt_ref, bias_ref, out_ref, s_scr, b, j, n_groups):
    qa = qa_ref[b, 0]
    fold = (MOBA_BLOCK // SUBLANES, SUBLANES, MOBA_BLOCK)
    first_near = (n_groups - 2) * ATTN_GROUP

    def qk_group(g):
        gmax = None
        for u in range(ATTN_GROUP):
            n = g * ATTN_GROUP + u
            s = _dot(ka_ref[b, n * MOBA_BLOCK:(n + 1) * MOBA_BLOCK, :], qa)
            if n >= first_near:
                s = s + bias_ref[jnp.clip(j - n, 0, N_BIAS_TILES - 1), 0]
            s_scr[b, g % 2, u] = s
            tmax = jnp.max(s.reshape(fold), axis=0)
            gmax = tmax if gmax is None else jnp.maximum(gmax, tmax)
        return gmax

    m = jnp.full((1, MOBA_BLOCK), NEG, F32)
    lrun = jnp.zeros((SUBLANES, MOBA_BLOCK), F32)
    acc = jnp.zeros((HEAD_DIM, MOBA_BLOCK), F32)
    gmax = qk_group(0)
    yield
    for g in range(n_groups):
        nxt = qk_group(g + 1) if g + 1 < n_groups else None
        yield
        m_new = jnp.maximum(m, jnp.max(gmax, axis=0, keepdims=True))
        alpha = jnp.exp2(m - m_new)
        lsum = jnp.zeros((SUBLANES, MOBA_BLOCK), F32)
        pv = jnp.zeros((HEAD_DIM, MOBA_BLOCK), F32)
        for u in range(ATTN_GROUP):
            p = jnp.exp2(s_scr[b, g % 2, u] - m_new)
            lsum = lsum + jnp.sum(p.reshape(fold), axis=0)
            pv = pv + _dot(vt_ref[b, 0, g * ATTN_GROUP + u], p.astype(BF16))
        lrun = alpha * lrun + lsum
        acc = alpha * acc + pv
        m = m_new
        gmax = nxt
        yield
    l = jnp.sum(lrun, axis=0, keepdims=True)
    out_ref[b] = (acc / l).T.astype(out_ref.dtype)


def _moba_attn_kernel(qa_ref, ka_ref, vt_ref, bias_ref, out_ref, s_scr):
    j = pl.program_id(1)
    bsz = qa_ref.shape[0]
    nb = vt_ref.shape[2]
    for n_groups in range(1, nb // ATTN_GROUP + 1):
        @pl.when(j // ATTN_GROUP == n_groups - 1)
        def _():
            _run_staggered([_attn_tile(qa_ref, ka_ref, vt_ref, bias_ref, out_ref, s_scr, b, j, n_groups)
                            for b in range(bsz)])


def _moba_attn(qa, ka, vt, bias):
    bsz, seq, _ = ka.shape
    nb = seq // MOBA_BLOCK
    assert nb % ATTN_GROUP == 0
    width = 2 * HEAD_DIM
    return pl.pallas_call(
        _moba_attn_kernel,
        grid=(HEADS, nb),
        in_specs=[
            pl.BlockSpec((bsz, 1, width, MOBA_BLOCK), lambda h, j: (0, h, 0, j)),
            pl.BlockSpec((bsz, seq, width), lambda h, j: (0, 0, h)),
            pl.BlockSpec((bsz, 1, nb, HEAD_DIM, MOBA_BLOCK), lambda h, j: (0, h, 0, 0, 0)),
            pl.BlockSpec((N_BIAS_TILES, 1, MOBA_BLOCK, MOBA_BLOCK), lambda h, j: (0, h, 0, 0)),
        ],
        out_specs=pl.BlockSpec((bsz, MOBA_BLOCK, HEAD_DIM), lambda h, j: (0, j, h)),
        out_shape=jax.ShapeDtypeStruct((bsz, seq, D_MODEL), BF16),
        scratch_shapes=[pltpu.VMEM((bsz, 2, ATTN_GROUP, MOBA_BLOCK, MOBA_BLOCK), F32)],
        compiler_params=pltpu.CompilerParams(
            dimension_semantics=("arbitrary", "arbitrary"), vmem_limit_bytes=VMEM_LIMIT),
        name="moba_attn",
    )(qa, ka, vt, bias)


def kernel(x, norm_mix, norm_ffn, hgrn_w_in, hgrn_lb_logits, hgrn_out_norm, hgrn_w_out,
           moba_w_in, moba_w_out, rel_bias_table, ffn_w13, ffn_w2, final_norm):
    depth = norm_mix.shape[0]
    n_mixers = 2
    hgrn_w_in, hgrn_w_out, moba_w_in, moba_w_out, ffn_w13, ffn_w2 = (
        w.astype(BF16) for w in (hgrn_w_in, hgrn_w_out, moba_w_in, moba_w_out, ffn_w13, ffn_w2))
    norm_mix, norm_ffn, hgrn_out_norm = (g[:, None, :] for g in (norm_mix, norm_ffn, hgrn_out_norm))
    bias = _bias_tiles(rel_bias_table)
    h = x
    for layer in range(depth):
        idx = layer // n_mixers
        if layer % n_mixers == 0:
            a = _hgrn_mixer(h, norm_mix, layer, hgrn_w_in, hgrn_lb_logits, hgrn_out_norm, idx)
            w_out = hgrn_w_out
        else:
            qa, ka, vt = _moba_proj(h, norm_mix, layer, moba_w_in, idx, rel_bias_table)
            a = _moba_attn(qa, ka, vt, bias)
            w_out = moba_w_out
        h = _outproj_ffn(h, a, w_out, idx, norm_ffn, ffn_w13, ffn_w2, layer, final_norm,
                         final_norm=(layer == depth - 1))
    return h
```

```python
import functools
import math

import numpy as np
import jax
import jax.numpy as jnp
from jax import lax
from jax.experimental import pallas as pl
from jax.experimental.pallas import tpu as pltpu

D_MODEL = 1024
HEADS = 8
HEAD_DIM = D_MODEL // HEADS
MOBA_BLOCK = 256
MOBA_TOPK = 3
REL_BUCKETS = 32
REL_MAX_DISTANCE = 1024
RMS_EPS = 1e-6
HGRN_CHUNK = 128
FFN_ROWS = 512
ATTN_GROUP = 2
NEG = -1e30
N_BIAS_TILES = 6
VMEM_LIMIT = 56 * 1024 * 1024
LOG2E = math.log2(math.e)
SUBLANES = 8

BF16 = jnp.bfloat16
F32 = jnp.float32


def _resident(shape):
    zeros = (0,) * len(shape)
    return pl.BlockSpec(shape, lambda *_: zeros, pipeline_mode=pl.Buffered(1))


def _layer(stacked, idx):
    zeros = (0,) * (stacked.ndim - 1)
    return pl.BlockSpec((None,) + stacked.shape[1:], lambda *_: (idx,) + zeros, pipeline_mode=pl.Buffered(1))


def _sigmoid(x):
    return 1.0 / (1.0 + jnp.exp(-x))


def _rms(x, gain):
    return x * lax.rsqrt(jnp.mean(x * x, axis=-1, keepdims=True) + RMS_EPS) * gain


def _dot(a, b):
    return jnp.dot(a, b, preferred_element_type=F32)


def _run_staggered(sequences):
    live = list(enumerate(sequences))
    step = 0
    while live:
        for i, seq in list(live):
            if step >= i and next(seq, StopIteration) is StopIteration:
                live.remove((i, seq))
        step += 1


def _hgrn_tables(c):
    ms = [c >> (i + 1) for i in range(int(math.log2(c)))]
    r = np.arange(c)
    level = np.full((c, c), -1, np.int32)
    gsel = np.zeros((len(ms), c, c), np.float32)
    sgn = np.zeros((len(ms), c, HEAD_DIM), np.float32)
    for i, m in enumerate(ms):
        upper = (r % (2 * m)) >= m
        same = (r[:, None] // (2 * m)) == (r[None, :] // (2 * m))
        level[same & upper[:, None] & ~upper[None, :]] = i
        mid = (r // (2 * m)) * 2 * m + m - 1
        gsel[i, r, mid] = 1.0
        sgn[i] = np.where(upper, 1.0, -1.0)[:, None]
    level[r, r] = len(ms)
    tri = (r[:, None] >= r[None, :]).astype(np.float32)
    fine = [i for i, m in enumerate(ms) if 2 * m < SUBLANES]
    return (jnp.asarray(tri, BF16), jnp.asarray(gsel[fine].reshape(len(fine) * c, c), BF16), jnp.asarray(sgn, F32),
            jnp.asarray(level, jnp.int32), ms)


def _hgrn_kernel(x_ref, gain_ref, w_ref, lbl_ref, onorm_ref, tri_ref, gsel_ref, sgn_ref, lvl_ref,
                 a_ref, st_ref, *, layer_idx, ms):
    bsz, c, d = x_ref.shape
    n_levels = len(ms)

    @pl.when(pl.program_id(0) == 0)
    def _():
        st_ref[...] = jnp.zeros_like(st_ref)

    logits = [lbl_ref[r:r + 1, :] for r in range(lbl_ref.shape[0])]
    top = functools.reduce(jnp.maximum, logits)
    ex = [jnp.exp(t - top) for t in logits]
    lb = sum(ex[:layer_idx + 1]) / sum(ex)

    xn = _rms(x_ref[...].reshape(bsz * c, d), gain_ref[...]).astype(BF16)
    p_all = _dot(xn, w_ref[...])

    def chunk(b):
        p = p_all[b * c:(b + 1) * c]
        qz = p[:, :d]
        q = qz * _sigmoid(qz)
        f = lb + (1.0 - lb) * _sigmoid(p[:, d:2 * d])
        k = 1.0 - f
        g = jnp.log(f) * LOG2E
        inp_b = p[:, 2 * d:3 * d].astype(BF16)
        og = _sigmoid(p[:, 3 * d:])
        g0 = g.astype(BF16)
        r1 = g - g0.astype(F32)
        g1 = r1.astype(BF16)
        g2 = (r1 - g1.astype(F32)).astype(BF16)
        yield
        tri = tri_ref[...]
        cum = _dot(tri, g0) + _dot(tri, g1) + _dot(tri, g2)
        fine_refs = _dot(gsel_ref[...], cum.astype(BF16))
        yield
        qe, ke = [], []
        for i, m in enumerate(ms):
            if 2 * m >= SUBLANES:
                blocks = cum.reshape(c // (2 * m), 2 * m, d)
                ref = jnp.broadcast_to(blocks[:, m - 1:m, :], blocks.shape).reshape(c, d)
            else:
                fi = i - (n_levels - gsel_ref.shape[0] // c)
                ref = fine_refs[fi * c:(fi + 1) * c]
            e = jnp.exp2((cum - ref) * jnp.tile(sgn_ref[i], (1, HEADS)))
            qe.append((q * e).astype(BF16))
            ke.append((k * e).T.astype(BF16))
        qe.append(q.astype(BF16))
        ke.append(k.T.astype(BF16))
        last = cum[c - 1:c, :]
        qs = (q * jnp.exp2(cum)).astype(BF16)
        kd = (k * jnp.exp2(last - cum)).T.astype(BF16)
        decay = jnp.broadcast_to(jnp.exp2(last), (SUBLANES, d)).T[:, 0:1]
        yield
        lvl = lvl_ref[...]
        outs = []
        for h in range(HEADS):
            cols = slice(h * HEAD_DIM, (h + 1) * HEAD_DIM)
            scores = jnp.zeros((c, c), F32)
            for i in range(n_levels + 1):
                scores = jnp.where(lvl == i, _dot(qe[i][:, cols], ke[i][cols, :]), scores)
            st = st_ref[b, h]
            o = _dot(scores.astype(BF16), inp_b[:, cols]) + _dot(qs[:, cols], st.astype(BF16))
            st_ref[b, h] = st * decay[cols, :] + _dot(kd[cols, :], inp_b[:, cols])
            outs.append(o)
        yield
        ss = sum(jnp.sum(o * o, axis=-1, keepdims=True) for o in outs)
        inv = lax.rsqrt(ss / D_MODEL + RMS_EPS)
        for h in range(HEADS):
            cols = slice(h * HEAD_DIM, (h + 1) * HEAD_DIM)
            a_ref[b, :, cols] = (outs[h] * inv * onorm_ref[:, cols] * og[:, cols]).astype(a_ref.dtype)

    _run_staggered([chunk(b) for b in range(bsz)])


def _hgrn_mixer(x, gains, layer, w_in, lb_logits, out_norms, layer_idx):
    bsz, seq, _ = x.shape
    c = HGRN_CHUNK
    tri, gsel, sgn, lvl, ms = _hgrn_tables(c)
    kern = functools.partial(_hgrn_kernel, layer_idx=layer_idx, ms=tuple(ms))
    return pl.pallas_call(
        kern,
        grid=(seq // c,),
        in_specs=[
            pl.BlockSpec((bsz, c, D_MODEL), lambda i: (0, i, 0)),
            _layer(gains, layer),
            _layer(w_in, layer_idx),
            _resident(lb_logits.shape),
            _layer(out_norms, layer_idx),
            _resident(tri.shape), _resident(gsel.shape), _resident(sgn.shape), _resident(lvl.shape),
        ],
        out_specs=pl.BlockSpec((bsz, c, D_MODEL), lambda i: (0, i, 0)),
        out_shape=jax.ShapeDtypeStruct((bsz, seq, D_MODEL), BF16),
        scratch_shapes=[pltpu.VMEM((bsz, HEADS, HEAD_DIM, HEAD_DIM), F32)],
        compiler_params=pltpu.CompilerParams(
            dimension_semantics=("arbitrary",), vmem_limit_bytes=VMEM_LIMIT),
        name="hgrn_mixer",
    )(x, gains, w_in, lb_logits, out_norms, tri, gsel, sgn, lvl)


def _ffn_kernel(res_ref, a_ref, wo_ref, gain_ref, w13_ref, w2_ref, fin_ref, out_ref, *, final_norm):
    d_ff = w2_ref.shape[0]
    h = res_ref[...] + _dot(a_ref[...], wo_ref[...])
    xn = _rms(h, gain_ref[...]).astype(BF16)
    gu = _dot(xn, w13_ref[...])
    g = gu[:, :d_ff]
    u = gu[:, d_ff:]
    act = (g * _sigmoid(g) * u).astype(BF16)
    h = h + _dot(act, w2_ref[...])
    if final_norm:
        h = _rms(h, fin_ref[...])
    out_ref[...] = h


def _outproj_ffn(res, a, w_out, idx, gains, w13, w2, layer, final_gain, final_norm):
    bsz, seq, _ = res.shape
    rows = bsz * seq
    kern = functools.partial(_ffn_kernel, final_norm=final_norm)
    row_spec = pl.BlockSpec((FFN_ROWS, D_MODEL), lambda i: (i, 0))
    out = pl.pallas_call(
        kern,
        grid=(rows // FFN_ROWS,),
        in_specs=[
            row_spec, row_spec,
            _layer(w_out, idx),
            _layer(gains, layer),
            _layer(w13, layer),
            _layer(w2, layer),
            _resident((1, D_MODEL)),
        ],
        out_specs=row_spec,
        out_shape=jax.ShapeDtypeStruct((rows, D_MODEL), F32),
        compiler_params=pltpu.CompilerParams(
            dimension_semantics=("arbitrary",), vmem_limit_bytes=VMEM_LIMIT),
        name="outproj_ffn_final" if final_norm else "outproj_ffn",
    )(res.reshape(rows, D_MODEL), a.reshape(rows, D_MODEL), w_out, gains, w13, w2, final_gain.reshape(1, D_MODEL))
    return out.reshape(bsz, seq, D_MODEL)


def _t5_thresholds():
    max_exact = REL_BUCKETS // 2
    n_log = REL_BUCKETS - max_exact
    ratio = REL_MAX_DISTANCE / max_exact
    return [int(math.ceil(max_exact * ratio ** (k / n_log) - 1e-9)) for k in range(1, n_log)]


def _t5_bucket(dist):
    max_exact = REL_BUCKETS // 2
    return dist if dist < max_exact else max_exact + sum(dist >= t for t in _t5_thresholds())


def _bias_kernel(tab_ref, out_ref):
    dl = pl.program_id(0)
    tk = lax.broadcasted_iota(jnp.int32, (MOBA_BLOCK, MOBA_BLOCK), 0)
    tq = lax.broadcasted_iota(jnp.int32, (MOBA_BLOCK, MOBA_BLOCK), 1)
    max_exact = REL_BUCKETS // 2
    for tile in range(N_BIAS_TILES - 1):
        @pl.when(dl == tile)
        def _():
            dist = tile * MOBA_BLOCK + tq - tk
            d_lo = max(tile * MOBA_BLOCK - (MOBA_BLOCK - 1), 0)
            d_hi = tile * MOBA_BLOCK + MOBA_BLOCK - 1
            b_lo, b_hi = _t5_bucket(d_lo), _t5_bucket(d_hi)
            large = jnp.full(dist.shape, max_exact, jnp.int32)
            for t in _t5_thresholds():
                if t <= d_hi:
                    large = large + (dist >= t).astype(jnp.int32)
            bucket = jnp.where(dist < max_exact, dist, large) if d_lo < max_exact else large
            for h in range(HEADS):
                bias = jnp.full(dist.shape, tab_ref[b_lo, h] * LOG2E, F32)
                for bkt in range(b_lo + 1, b_hi + 1):
                    bias = jnp.where(bucket == bkt, tab_ref[bkt, h] * LOG2E, bias)
                if tile == 0:
                    bias = jnp.where(dist < 0, NEG, bias)
                out_ref[0, h] = bias

    @pl.when(dl == N_BIAS_TILES - 1)
    def _():
        out_ref[...] = jnp.zeros_like(out_ref)


def _bias_tiles(rel_table):
    return pl.pallas_call(
        _bias_kernel,
        grid=(N_BIAS_TILES,),
        in_specs=[pl.BlockSpec(memory_space=pltpu.SMEM)],
        out_specs=pl.BlockSpec((1, HEADS, MOBA_BLOCK, MOBA_BLOCK), lambda i: (i, 0, 0, 0)),
        out_shape=jax.ShapeDtypeStruct((N_BIAS_TILES, HEADS, MOBA_BLOCK, MOBA_BLOCK), F32),
        compiler_params=pltpu.CompilerParams(dimension_semantics=("arbitrary",)),
        name="moba_bias_tiles",
    )(rel_table)


def _moba_proj_kernel(x_ref, gain_ref, w_ref, tab_ref, qa_ref, ka_ref, vt_ref, kmean_scr):
    j = pl.program_id(0)
    bsz, nb = kmean_scr.shape[0], kmean_scr.shape[2]
    width = 2 * HEAD_DIM
    col_head = lax.broadcasted_iota(jnp.int32, (1, D_MODEL), 1) // HEAD_DIM

    @pl.when(j == 0)
    def _():
        kmean_scr[...] = jnp.zeros_like(kmean_scr)

    xn = _rms(x_ref[...].reshape(bsz * MOBA_BLOCK, D_MODEL), gain_ref[...]).astype(BF16)
    qkv = _dot(xn, w_ref[...])

    lane = lax.broadcasted_iota(jnp.int32, (MOBA_BLOCK, HEAD_DIM), 1)
    onehot = ((lane == j) | (lane == j + nb)).astype(BF16)
    blk = lax.broadcasted_iota(jnp.int32, (HEADS, nb, MOBA_BLOCK), 1)
    blk_f = blk.astype(F32)
    far_bias = jnp.concatenate(
        [jnp.broadcast_to(tab_ref[REL_BUCKETS - 1:REL_BUCKETS, h:h + 1] * LOG2E, (1, nb, MOBA_BLOCK)) for h in range(HEADS)],
        axis=0)
    far_bias = jnp.where(j - blk >= N_BIAS_TILES - 1, far_bias, 0.0)
    pad = jnp.zeros((HEAD_DIM - 2 * nb, MOBA_BLOCK), F32)
    for b in range(bsz):
        rows = slice(b * MOBA_BLOCK, (b + 1) * MOBA_BLOCK)
        q = qkv[rows, :D_MODEL]
        k = qkv[rows, D_MODEL:2 * D_MODEL]
        v = qkv[rows, 2 * D_MODEL:]
        vt_ref[b, :, 0] = v.T.reshape(HEADS, HEAD_DIM, MOBA_BLOCK).astype(BF16)
        qt_all = q.T
        km = kmean_scr[b].reshape(HEADS * nb, D_MODEL)
        km_hi = km.astype(BF16)
        km_lo = (km - km_hi.astype(F32)).astype(BF16)
        qt_hi = qt_all.astype(BF16)
        qt_lo = (qt_all - qt_hi.astype(F32)).astype(BF16)
        gate = (_dot(km_hi, qt_hi) + _dot(km_hi, qt_lo) + _dot(km_lo, qt_hi)).reshape(HEADS, nb, MOBA_BLOCK)
        gate = jnp.where(blk < j, gate, -jnp.inf)
        madd = jnp.full(gate.shape, NEG, F32)
        for _ in range(MOBA_TOPK):
            top = jnp.max(gate, axis=1, keepdims=True)
            idx = jnp.min(jnp.where(gate == top, blk_f, float(nb)), axis=1, keepdims=True)
            hit = blk_f == idx
            madd = jnp.where(hit & (top > -jnp.inf), far_bias, madd)
            gate = jnp.where(hit, -jnp.inf, gate)
        madd = jnp.where(blk == j, 0.0, madd)
        hi = madd.astype(BF16).astype(F32)
        lo = (madd - hi).astype(BF16).astype(F32)
        for h in range(HEADS):
            cols = slice(h * HEAD_DIM, (h + 1) * HEAD_DIM)
            qt = qt_all[cols, :] * (LOG2E * HEAD_DIM ** -0.5)
            qa_ref[b, h] = jnp.concatenate([qt, hi[h], lo[h], pad], axis=0).astype(BF16)
            ka_ref[b, :, h * width:h * width + HEAD_DIM] = k[:, cols].astype(BF16)
            ka_ref[b, :, h * width + HEAD_DIM:(h + 1) * width] = onehot
        kmean = jnp.mean(k, axis=0, keepdims=True)
        for h in range(HEADS):
            kmean_scr[b, h, pl.ds(j, 1), :] = jnp.where(col_head == h, kmean, 0.0)


def _moba_proj(h, gains, layer, w_in, idx, rel_table):
    bsz, seq, _ = h.shape
    nb = seq // MOBA_BLOCK
    assert 2 * nb <= HEAD_DIM
    return pl.pallas_call(
        _moba_proj_kernel,
        grid=(nb,),
        in_specs=[
            pl.BlockSpec((bsz, MOBA_BLOCK, D_MODEL), lambda j: (0, j, 0)),
            _layer(gains, layer),
            _layer(w_in, idx),
            _resident(rel_table.shape),
        ],
        out_specs=[
            pl.BlockSpec((bsz, HEADS, 2 * HEAD_DIM, MOBA_BLOCK), lambda j: (0, 0, 0, j)),
            pl.BlockSpec((bsz, MOBA_BLOCK, 2 * D_MODEL), lambda j: (0, j, 0)),
            pl.BlockSpec((bsz, HEADS, 1, HEAD_DIM, MOBA_BLOCK), lambda j: (0, 0, j, 0, 0)),
        ],
        out_shape=[
            jax.ShapeDtypeStruct((bsz, HEADS, 2 * HEAD_DIM, seq), BF16),
            jax.ShapeDtypeStruct((bsz, seq, 2 * D_MODEL), BF16),
            jax.ShapeDtypeStruct((bsz, HEADS, nb, HEAD_DIM, MOBA_BLOCK), BF16),
        ],
        scratch_shapes=[pltpu.VMEM((bsz, HEADS, nb, D_MODEL), F32)],
        compiler_params=pltpu.CompilerParams(
            dimension_semantics=("arbitrary",), vmem_limit_bytes=VMEM_LIMIT),
        name="moba_proj",
    )(h, gains, w_in, rel_table)


def _attn_tile(qa_ref, ka_ref, vt_ref, bias_ref, out_ref, s_scr, b, j, n_groups):
    qa = qa_ref[b, 0]
    fold = (MOBA_BLOCK // SUBLANES, SUBLANES, MOBA_BLOCK)
    first_near = (n_groups - 1) * ATTN_GROUP - (N_BIAS_TILES - 2)

    def qk_group(g):
        gmax = None
        for u in range(ATTN_GROUP):
            n = g * ATTN_GROUP + u
            s = _dot(ka_ref[b, n * MOBA_BLOCK:(n + 1) * MOBA_BLOCK, :], qa)
            if n >= first_near:
                s = s + bias_ref[jnp.clip(j - n, 0, N_BIAS_TILES - 1), 0]
            s_scr[b, g % 2, u] = s
            tmax = jnp.max(s.reshape(fold), axis=0)
            gmax = tmax if gmax is None else jnp.maximum(gmax, tmax)
        return gmax

    m = jnp.full((1, MOBA_BLOCK), NEG, F32)
    lrun = jnp.zeros((SUBLANES, MOBA_BLOCK), F32)
    acc = jnp.zeros((HEAD_DIM, MOBA_BLOCK), F32)
    gmax = qk_group(0)
    yield
    for g in range(n_groups):
        nxt = qk_group(g + 1) if g + 1 < n_groups else None
        yield
        m_new = jnp.maximum(m, jnp.max(gmax, axis=0, keepdims=True))
        alpha = jnp.exp2(m - m_new)
        lsum = jnp.zeros((SUBLANES, MOBA_BLOCK), F32)
        pv = jnp.zeros((HEAD_DIM, MOBA_BLOCK), F32)
        for u in range(ATTN_GROUP):
            p = jnp.exp2(s_scr[b, g % 2, u] - m_new)
            lsum = lsum + jnp.sum(p.reshape(fold), axis=0)
            pv = pv + _dot(vt_ref[b, 0, g * ATTN_GROUP + u], p.astype(BF16))
        lrun = alpha * lrun + lsum
        acc = alpha * acc + pv
        m = m_new
        gmax = nxt
        yield
    l = jnp.sum(lrun, axis=0, keepdims=True)
    out_ref[b] = (acc / l).T.astype(out_ref.dtype)


def _moba_attn_kernel(qa_ref, ka_ref, vt_ref, bias_ref, out_ref, s_scr):
    j = pl.program_id(1)
    bsz = qa_ref.shape[0]
    nb = vt_ref.shape[2]
    for n_groups in range(1, nb // ATTN_GROUP + 1):
        @pl.when(j // ATTN_GROUP == n_groups - 1)
        def _():
            _run_staggered([_attn_tile(qa_ref, ka_ref, vt_ref, bias_ref, out_ref, s_scr, b, j, n_groups)
                            for b in range(bsz)])


def _moba_attn(qa, ka, vt, bias):
    bsz, seq, _ = ka.shape
    nb = seq // MOBA_BLOCK
    assert nb % ATTN_GROUP == 0
    width = 2 * HEAD_DIM
    return pl.pallas_call(
        _moba_attn_kernel,
        grid=(HEADS, nb),
        in_specs=[
            pl.BlockSpec((bsz, 1, width, MOBA_BLOCK), lambda h, j: (0, h, 0, j)),
            pl.BlockSpec((bsz, seq, width), lambda h, j: (0, 0, h)),
            pl.BlockSpec((bsz, 1, nb, HEAD_DIM, MOBA_BLOCK), lambda h, j: (0, h, 0, 0, 0)),
            pl.BlockSpec((N_BIAS_TILES, 1, MOBA_BLOCK, MOBA_BLOCK), lambda h, j: (0, h, 0, 0)),
        ],
        out_specs=pl.BlockSpec((bsz, MOBA_BLOCK, HEAD_DIM), lambda h, j: (0, j, h)),
        out_shape=jax.ShapeDtypeStruct((bsz, seq, D_MODEL), BF16),
        scratch_shapes=[pltpu.VMEM((bsz, 2, ATTN_GROUP, MOBA_BLOCK, MOBA_BLOCK), F32)],
        compiler_params=pltpu.CompilerParams(
            dimension_semantics=("arbitrary", "arbitrary"), vmem_limit_bytes=VMEM_LIMIT),
        name="moba_attn",
    )(qa, ka, vt, bias)


def kernel(x, norm_mix, norm_ffn, hgrn_w_in, hgrn_lb_logits, hgrn_out_norm, hgrn_w_out,
           moba_w_in, moba_w_out, rel_bias_table, ffn_w13, ffn_w2, final_norm):
    depth = norm_mix.shape[0]
    n_mixers = 2
    hgrn_w_in, hgrn_w_out, moba_w_in, moba_w_out, ffn_w13, ffn_w2 = (
        w.astype(BF16) for w in (hgrn_w_in, hgrn_w_out, moba_w_in, moba_w_out, ffn_w13, ffn_w2))
    norm_mix, norm_ffn, hgrn_out_norm = (g[:, None, :] for g in (norm_mix, norm_ffn, hgrn_out_norm))
    bias = _bias_tiles(rel_bias_table)
    h = x
    for layer in range(depth):
        idx = layer // n_mixers
        if layer % n_mixers == 0:
            a = _hgrn_mixer(h, norm_mix, layer, hgrn_w_in, hgrn_lb_logits, hgrn_out_norm, idx)
            w_out = hgrn_w_out
        else:
            qa, ka, vt = _moba_proj(h, norm_mix, layer, moba_w_in, idx, rel_bias_table)
            a = _moba_attn(qa, ka, vt, bias)
            w_out = moba_w_out
        h = _outproj_ffn(h, a, w_out, idx, norm_ffn, ffn_w13, ffn_w2, layer, final_norm,
                         final_norm=(layer == depth - 1))
    return h
```

```python
import functools
import math

import numpy as np
import jax
import jax.numpy as jnp
from jax import lax
from jax.experimental import pallas as pl
from jax.experimental.pallas import tpu as pltpu

D_MODEL = 1024
HEADS = 8
HEAD_DIM = D_MODEL // HEADS
MOBA_BLOCK = 256
MOBA_TOPK = 3
REL_BUCKETS = 32
REL_MAX_DISTANCE = 1024
RMS_EPS = 1e-6
HGRN_CHUNK = 128
FFN_ROWS = 512
ATTN_GROUP = 4
NEG = -1e30
N_BIAS_TILES = 6
VMEM_LIMIT = 56 * 1024 * 1024
LOG2E = math.log2(math.e)
SUBLANES = 8

BF16 = jnp.bfloat16
F32 = jnp.float32


def _resident(shape):
    zeros = (0,) * len(shape)
    return pl.BlockSpec(shape, lambda *_: zeros, pipeline_mode=pl.Buffered(1))


def _layer(stacked, idx):
    zeros = (0,) * (stacked.ndim - 1)
    return pl.BlockSpec((None,) + stacked.shape[1:], lambda *_: (idx,) + zeros, pipeline_mode=pl.Buffered(1))


def _sigmoid(x):
    return 1.0 / (1.0 + jnp.exp(-x))


def _rms(x, gain):
    return x * lax.rsqrt(jnp.mean(x * x, axis=-1, keepdims=True) + RMS_EPS) * gain


def _dot(a, b):
    return jnp.dot(a, b, preferred_element_type=F32)


def _run_staggered(sequences):
    live = list(enumerate(sequences))
    step = 0
    while live:
        for i, seq in list(live):
            if step >= i and next(seq, StopIteration) is StopIteration:
                live.remove((i, seq))
        step += 1


def _hgrn_tables(c):
    ms = [c >> (i + 1) for i in range(int(math.log2(c)))]
    r = np.arange(c)
    level = np.full((c, c), -1, np.int32)
    gsel = np.zeros((len(ms), c, c), np.float32)
    sgn = np.zeros((len(ms), c, HEAD_DIM), np.float32)
    for i, m in enumerate(ms):
        upper = (r % (2 * m)) >= m
        same = (r[:, None] // (2 * m)) == (r[None, :] // (2 * m))
        level[same & upper[:, None] & ~upper[None, :]] = i
        mid = (r // (2 * m)) * 2 * m + m - 1
        gsel[i, r, mid] = 1.0
        sgn[i] = np.where(upper, 1.0, -1.0)[:, None]
    level[r, r] = len(ms)
    tri = (r[:, None] >= r[None, :]).astype(np.float32)
    fine = [i for i, m in enumerate(ms) if 2 * m < SUBLANES]
    return (jnp.asarray(tri, BF16), jnp.asarray(gsel[fine].reshape(len(fine) * c, c), BF16), jnp.asarray(sgn, F32),
            jnp.asarray(level, jnp.int32), ms)


def _hgrn_kernel(x_ref, gain_ref, w_ref, lbl_ref, onorm_ref, tri_ref, gsel_ref, sgn_ref, lvl_ref,
                 a_ref, st_ref, *, layer_idx, ms):
    bsz, c, d = x_ref.shape
    n_levels = len(ms)

    @pl.when(pl.program_id(0) == 0)
    def _():
        st_ref[...] = jnp.zeros_like(st_ref)

    logits = [lbl_ref[r:r + 1, :] for r in range(lbl_ref.shape[0])]
    top = functools.reduce(jnp.maximum, logits)
    ex = [jnp.exp(t - top) for t in logits]
    lb = sum(ex[:layer_idx + 1]) / sum(ex)

    xn = _rms(x_ref[...].reshape(bsz * c, d), gain_ref[...]).astype(BF16)
    p_all = _dot(xn, w_ref[...])

    def chunk(b):
        p = p_all[b * c:(b + 1) * c]
        qz = p[:, :d]
        q = qz * _sigmoid(qz)
        f = lb + (1.0 - lb) * _sigmoid(p[:, d:2 * d])
        k = 1.0 - f
        g = jnp.log(f) * LOG2E
        inp_b = p[:, 2 * d:3 * d].astype(BF16)
        og = _sigmoid(p[:, 3 * d:])
        g0 = g.astype(BF16)
        r1 = g - g0.astype(F32)
        g1 = r1.astype(BF16)
        g2 = (r1 - g1.astype(F32)).astype(BF16)
        yield
        tri = tri_ref[...]
        cum = _dot(tri, g0) + _dot(tri, g1) + _dot(tri, g2)
        fine_refs = _dot(gsel_ref[...], cum.astype(BF16))
        yield
        qe, ke = [], []
        for i, m in enumerate(ms):
            if 2 * m >= SUBLANES:
                blocks = cum.reshape(c // (2 * m), 2 * m, d)
                ref = jnp.broadcast_to(blocks[:, m - 1:m, :], blocks.shape).reshape(c, d)
            else:
                fi = i - (n_levels - gsel_ref.shape[0] // c)
                ref = fine_refs[fi * c:(fi + 1) * c]
            e = jnp.exp2((cum - ref) * jnp.tile(sgn_ref[i], (1, HEADS)))
            qe.append((q * e).astype(BF16))
            ke.append((k * e).T.astype(BF16))
        qe.append(q.astype(BF16))
        ke.append(k.T.astype(BF16))
        last = cum[c - 1:c, :]
        qs = (q * jnp.exp2(cum)).astype(BF16)
        kd = (k * jnp.exp2(last - cum)).T.astype(BF16)
        decay = jnp.broadcast_to(jnp.exp2(last), (SUBLANES, d)).T[:, 0:1]
        yield
        lvl = lvl_ref[...]
        outs = []
        for h in range(HEADS):
            cols = slice(h * HEAD_DIM, (h + 1) * HEAD_DIM)
            scores = jnp.zeros((c, c), F32)
            for i in range(n_levels + 1):
                scores = jnp.where(lvl == i, _dot(qe[i][:, cols], ke[i][cols, :]), scores)
            st = st_ref[b, h]
            o = _dot(scores.astype(BF16), inp_b[:, cols]) + _dot(qs[:, cols], st.astype(BF16))
            st_ref[b, h] = st * decay[cols, :] + _dot(kd[cols, :], inp_b[:, cols])
            outs.append(o)
        yield
        ss = sum(jnp.sum(o * o, axis=-1, keepdims=True) for o in outs)
        inv = lax.rsqrt(ss / D_MODEL + RMS_EPS)
        for h in range(HEADS):
            cols = slice(h * HEAD_DIM, (h + 1) * HEAD_DIM)
            a_ref[b, :, cols] = (outs[h] * inv * onorm_ref[:, cols] * og[:, cols]).astype(a_ref.dtype)

    _run_staggered([chunk(b) for b in range(bsz)])


def _hgrn_mixer(x, gains, layer, w_in, lb_logits, out_norms, layer_idx):
    bsz, seq, _ = x.shape
    c = HGRN_CHUNK
    tri, gsel, sgn, lvl, ms = _hgrn_tables(c)
    kern = functools.partial(_hgrn_kernel, layer_idx=layer_idx, ms=tuple(ms))
    return pl.pallas_call(
        kern,
        grid=(seq // c,),
        in_specs=[
            pl.BlockSpec((bsz, c, D_MODEL), lambda i: (0, i, 0)),
            _layer(gains, layer),
            _layer(w_in, layer_idx),
            _resident(lb_logits.shape),
            _layer(out_norms, layer_idx),
            _resident(tri.shape), _resident(gsel.shape), _resident(sgn.shape), _resident(lvl.shape),
        ],
        out_specs=pl.BlockSpec((bsz, c, D_MODEL), lambda i: (0, i, 0)),
        out_shape=jax.ShapeDtypeStruct((bsz, seq, D_MODEL), BF16),
        scratch_shapes=[pltpu.VMEM((bsz, HEADS, HEAD_DIM, HEAD_DIM), F32)],
        compiler_params=pltpu.CompilerParams(
            dimension_semantics=("arbitrary",), vmem_limit_bytes=VMEM_LIMIT),
        name="hgrn_mixer",
    )(x, gains, w_in, lb_logits, out_norms, tri, gsel, sgn, lvl)


def _ffn_kernel(res_ref, a_ref, wo_ref, gain_ref, w13_ref, w2_ref, fin_ref, out_ref, *, final_norm):
    d_ff = w2_ref.shape[0]
    h = res_ref[...] + _dot(a_ref[...], wo_ref[...])
    xn = _rms(h, gain_ref[...]).astype(BF16)
    gu = _dot(xn, w13_ref[...])
    g = gu[:, :d_ff]
    u = gu[:, d_ff:]
    act = (g * _sigmoid(g) * u).astype(BF16)
    h = h + _dot(act, w2_ref[...])
    if final_norm:
        h = _rms(h, fin_ref[...])
    out_ref[...] = h


def _outproj_ffn(res, a, w_out, idx, gains, w13, w2, layer, final_gain, final_norm):
    bsz, seq, _ = res.shape
    rows = bsz * seq
    kern = functools.partial(_ffn_kernel, final_norm=final_norm)
    row_spec = pl.BlockSpec((FFN_ROWS, D_MODEL), lambda i: (i, 0))
    out = pl.pallas_call(
        kern,
        grid=(rows // FFN_ROWS,),
        in_specs=[
            row_spec, row_spec,
            _layer(w_out, idx),
            _layer(gains, layer),
            _layer(w13, layer),
            _layer(w2, layer),
            _resident((1, D_MODEL)),
        ],
        out_specs=row_spec,
        out_shape=jax.ShapeDtypeStruct((rows, D_MODEL), F32),
        compiler_params=pltpu.CompilerParams(
            dimension_semantics=("arbitrary",), vmem_limit_bytes=VMEM_LIMIT),
        name="outproj_ffn_final" if final_norm else "outproj_ffn",
    )(res.reshape(rows, D_MODEL), a.reshape(rows, D_MODEL), w_out, gains, w13, w2, final_gain.reshape(1, D_MODEL))
    return out.reshape(bsz, seq, D_MODEL)


def _t5_thresholds():
    max_exact = REL_BUCKETS // 2
    n_log = REL_BUCKETS - max_exact
    ratio = REL_MAX_DISTANCE / max_exact
    return [int(math.ceil(max_exact * ratio ** (k / n_log) - 1e-9)) for k in range(1, n_log)]


def _t5_bucket(dist):
    max_exact = REL_BUCKETS // 2
    return dist if dist < max_exact else max_exact + sum(dist >= t for t in _t5_thresholds())


def _bias_kernel(tab_ref, out_ref):
    dl = pl.program_id(0)
    tk = lax.broadcasted_iota(jnp.int32, (MOBA_BLOCK, MOBA_BLOCK), 0)
    tq = lax.broadcasted_iota(jnp.int32, (MOBA_BLOCK, MOBA_BLOCK), 1)
    max_exact = REL_BUCKETS // 2
    for tile in range(N_BIAS_TILES - 1):
        @pl.when(dl == tile)
        def _():
            dist = tile * MOBA_BLOCK + tq - tk
            d_lo = max(tile * MOBA_BLOCK - (MOBA_BLOCK - 1), 0)
            d_hi = tile * MOBA_BLOCK + MOBA_BLOCK - 1
            b_lo, b_hi = _t5_bucket(d_lo), _t5_bucket(d_hi)
            large = jnp.full(dist.shape, max_exact, jnp.int32)
            for t in _t5_thresholds():
                if t <= d_hi:
                    large = large + (dist >= t).astype(jnp.int32)
            bucket = jnp.where(dist < max_exact, dist, large) if d_lo < max_exact else large
            for h in range(HEADS):
                bias = jnp.full(dist.shape, tab_ref[b_lo, h] * LOG2E, F32)
                for bkt in range(b_lo + 1, b_hi + 1):
                    bias = jnp.where(bucket == bkt, tab_ref[bkt, h] * LOG2E, bias)
                if tile == 0:
                    bias = jnp.where(dist < 0, NEG, bias)
                out_ref[0, h] = bias

    @pl.when(dl == N_BIAS_TILES - 1)
    def _():
        out_ref[...] = jnp.zeros_like(out_ref)


def _bias_tiles(rel_table):
    return pl.pallas_call(
        _bias_kernel,
        grid=(N_BIAS_TILES,),
        in_specs=[pl.BlockSpec(memory_space=pltpu.SMEM)],
        out_specs=pl.BlockSpec((1, HEADS, MOBA_BLOCK, MOBA_BLOCK), lambda i: (i, 0, 0, 0)),
        out_shape=jax.ShapeDtypeStruct((N_BIAS_TILES, HEADS, MOBA_BLOCK, MOBA_BLOCK), F32),
        compiler_params=pltpu.CompilerParams(dimension_semantics=("arbitrary",)),
        name="moba_bias_tiles",
    )(rel_table)


def _moba_proj_kernel(x_ref, gain_ref, w_ref, tab_ref, qa_ref, ka_ref, vt_ref, kmean_scr):
    j = pl.program_id(0)
    bsz, nb = kmean_scr.shape[0], kmean_scr.shape[2]
    width = 2 * HEAD_DIM
    col_head = lax.broadcasted_iota(jnp.int32, (1, D_MODEL), 1) // HEAD_DIM

    @pl.when(j == 0)
    def _():
        kmean_scr[...] = jnp.zeros_like(kmean_scr)

    xn = _rms(x_ref[...].reshape(bsz * MOBA_BLOCK, D_MODEL), gain_ref[...]).astype(BF16)
    qkv = _dot(xn, w_ref[...])

    lane = lax.broadcasted_iota(jnp.int32, (MOBA_BLOCK, HEAD_DIM), 1)
    onehot = ((lane == j) | (lane == j + nb)).astype(BF16)
    blk = lax.broadcasted_iota(jnp.int32, (HEADS, nb, MOBA_BLOCK), 1)
    blk_f = blk.astype(F32)
    far_bias = jnp.concatenate(
        [jnp.broadcast_to(tab_ref[REL_BUCKETS - 1:REL_BUCKETS, h:h + 1] * LOG2E, (1, nb, MOBA_BLOCK)) for h in range(HEADS)],
        axis=0)
    far_bias = jnp.where(j - blk >= N_BIAS_TILES - 1, far_bias, 0.0)
    pad = jnp.zeros((HEAD_DIM - 2 * nb, MOBA_BLOCK), F32)
    for b in range(bsz):
        rows = slice(b * MOBA_BLOCK, (b + 1) * MOBA_BLOCK)
        q = qkv[rows, :D_MODEL]
        k = qkv[rows, D_MODEL:2 * D_MODEL]
        v = qkv[rows, 2 * D_MODEL:]
        vt_ref[b, :, 0] = v.T.reshape(HEADS, HEAD_DIM, MOBA_BLOCK).astype(BF16)
        qt_all = q.T
        km = kmean_scr[b].reshape(HEADS * nb, D_MODEL)
        km_hi = km.astype(BF16)
        km_lo = (km - km_hi.astype(F32)).astype(BF16)
        qt_hi = qt_all.astype(BF16)
        qt_lo = (qt_all - qt_hi.astype(F32)).astype(BF16)
        gate = (_dot(km_hi, qt_hi) + _dot(km_hi, qt_lo) + _dot(km_lo, qt_hi)).reshape(HEADS, nb, MOBA_BLOCK)
        gate = jnp.where(blk < j, gate, -jnp.inf)
        madd = jnp.full(gate.shape, NEG, F32)
        for _ in range(MOBA_TOPK):
            top = jnp.max(gate, axis=1, keepdims=True)
            idx = jnp.min(jnp.where(gate == top, blk_f, float(nb)), axis=1, keepdims=True)
            hit = blk_f == idx
            madd = jnp.where(hit & (top > -jnp.inf), far_bias, madd)
            gate = jnp.where(hit, -jnp.inf, gate)
        madd = jnp.where(blk == j, 0.0, madd)
        hi = madd.astype(BF16).astype(F32)
        lo = (madd - hi).astype(BF16).astype(F32)
        for h in range(HEADS):
            cols = slice(h * HEAD_DIM, (h + 1) * HEAD_DIM)
            qt = qt_all[cols, :] * (LOG2E * HEAD_DIM ** -0.5)
            qa_ref[b, h] = jnp.concatenate([qt, hi[h], lo[h], pad], axis=0).astype(BF16)
            ka_ref[b, :, h * width:h * width + HEAD_DIM] = k[:, cols].astype(BF16)
            ka_ref[b, :, h * width + HEAD_DIM:(h + 1) * width] = onehot
        kmean = jnp.mean(k, axis=0, keepdims=True)
        for h in range(HEADS):
            kmean_scr[b, h, pl.ds(j, 1), :] = jnp.where(col_head == h, kmean, 0.0)


def _moba_proj(h, gains, layer, w_in, idx, rel_table):
    bsz, seq, _ = h.shape
    nb = seq // MOBA_BLOCK
    assert 2 * nb <= HEAD_DIM
    return pl.pallas_call(
        _moba_proj_kernel,
        grid=(nb,),
        in_specs=[
            pl.BlockSpec((bsz, MOBA_BLOCK, D_MODEL), lambda j: (0, j, 0)),
            _layer(gains, layer),
            _layer(w_in, idx),
            _resident(rel_table.shape),
        ],
        out_specs=[
            pl.BlockSpec((bsz, HEADS, 2 * HEAD_DIM, MOBA_BLOCK), lambda j: (0, 0, 0, j)),
            pl.BlockSpec((bsz, MOBA_BLOCK, 2 * D_MODEL), lambda j: (0, j, 0)),
            pl.BlockSpec((bsz, HEADS, 1, HEAD_DIM, MOBA_BLOCK), lambda j: (0, 0, j, 0, 0)),
        ],
        out_shape=[
            jax.ShapeDtypeStruct((bsz, HEADS, 2 * HEAD_DIM, seq), BF16),
            jax.ShapeDtypeStruct((bsz, seq, 2 * D_MODEL), BF16),
            jax.ShapeDtypeStruct((bsz, HEADS, nb, HEAD_DIM, MOBA_BLOCK), BF16),
        ],
        scratch_shapes=[pltpu.VMEM((bsz, HEADS, nb, D_MODEL), F32)],
        compiler_params=pltpu.CompilerParams(
            dimension_semantics=("arbitrary",), vmem_limit_bytes=VMEM_LIMIT),
        name="moba_proj",
    )(h, gains, w_in, rel_table)


def _attn_tile(qa_ref, ka_ref, vt_ref, bias_ref, out_ref, s_scr, b, j, n_groups):
    qa = qa_ref[b, 0]
    fold = (MOBA_BLOCK // SUBLANES, SUBLANES, MOBA_BLOCK)
    first_near = (n_groups - 1) * ATTN_GROUP - (N_BIAS_TILES - 2)

    def qk_group(g):
        gmax = None
        for u in range(ATTN_GROUP):
            n = g * ATTN_GROUP + u
            s = _dot(ka_ref[b, n * MOBA_BLOCK:(n + 1) * MOBA_BLOCK, :], qa)
            if n >= first_near:
                s = s + bias_ref[jnp.clip(j - n, 0, N_BIAS_TILES - 1), 0]
            s_scr[b, g % 2, u] = s
            tmax = jnp.max(s.reshape(fold), axis=0)
            gmax = tmax if gmax is None else jnp.maximum(gmax, tmax)
        return gmax

    m = jnp.full((1, MOBA_BLOCK), NEG, F32)
    lrun = jnp.zeros((SUBLANES, MOBA_BLOCK), F32)
    acc = jnp.zeros((HEAD_DIM, MOBA_BLOCK), F32)
    gmax = qk_group(0)
    yield
    for g in range(n_groups):
        nxt = qk_group(g + 1) if g + 1 < n_groups else None
        yield
        m_new = jnp.maximum(m, jnp.max(gmax, axis=0, keepdims=True))
        alpha = jnp.exp2(m - m_new)
        lsum = jnp.zeros((SUBLANES, MOBA_BLOCK), F32)
        pv = jnp.zeros((HEAD_DIM, MOBA_BLOCK), F32)
        for u in range(ATTN_GROUP):
            p = jnp.exp2(s_scr[b, g % 2, u] - m_new)
            lsum = lsum + jnp.sum(p.reshape(fold), axis=0)
            pv = pv + _dot(vt_ref[b, 0, g * ATTN_GROUP + u], p.astype(BF16))
        lrun = alpha * lrun + lsum
        acc = alpha * acc + pv
        m = m_new
        gmax = nxt
        yield
    l = jnp.sum(lrun, axis=0, keepdims=True)
    out_ref[b] = (acc / l).T.astype(out_ref.dtype)


def _moba_attn_kernel(qa_ref, ka_ref, vt_ref, bias_ref, out_ref, s_scr):
    j = pl.program_id(1)
    bsz = qa_ref.shape[0]
    nb = vt_ref.shape[2]
    for n_groups in range(1, nb // ATTN_GROUP + 1):
        @pl.when(j // ATTN_GROUP == n_groups - 1)
        def _():
            _run_staggered([_attn_tile(qa_ref, ka_ref, vt_ref, bias_ref, out_ref, s_scr, b, j, n_groups)
                            for b in range(bsz)])


def _moba_attn(qa, ka, vt, bias):
    bsz, seq, _ = ka.shape
    nb = seq // MOBA_BLOCK
    assert nb % ATTN_GROUP == 0
    width = 2 * HEAD_DIM
    return pl.pallas_call(
        _moba_attn_kernel,
        grid=(HEADS, nb),
        in_specs=[
            pl.BlockSpec((bsz, 1, width, MOBA_BLOCK), lambda h, j: (0, h, 0, j)),
            pl.BlockSpec((bsz, seq, width), lambda h, j: (0, 0, h)),
            pl.BlockSpec((bsz, 1, nb, HEAD_DIM, MOBA_BLOCK), lambda h, j: (0, h, 0, 0, 0)),
            pl.BlockSpec((N_BIAS_TILES, 1, MOBA_BLOCK, MOBA_BLOCK), lambda h, j: (0, h, 0, 0)),
        ],
        out_specs=pl.BlockSpec((bsz, MOBA_BLOCK, HEAD_DIM), lambda h, j: (0, j, h)),
        out_shape=jax.ShapeDtypeStruct((bsz, seq, D_MODEL), BF16),
        scratch_shapes=[pltpu.VMEM((bsz, 2, ATTN_GROUP, MOBA_BLOCK, MOBA_BLOCK), F32)],
        compiler_params=pltpu.CompilerParams(
            dimension_semantics=("arbitrary", "arbitrary"), vmem_limit_bytes=VMEM_LIMIT),
        name="moba_attn",
    )(qa, ka, vt, bias)


def kernel(x, norm_mix, norm_ffn, hgrn_w_in, hgrn_lb_logits, hgrn_out_norm, hgrn_w_out,
           moba_w_in, moba_w_out, rel_bias_table, ffn_w13, ffn_w2, final_norm):
    depth = norm_mix.shape[0]
    n_mixers = 2
    hgrn_w_in, hgrn_w_out, moba_w_in, moba_w_out, ffn_w13, ffn_w2 = (
        w.astype(BF16) for w in (hgrn_w_in, hgrn_w_out, moba_w_in, moba_w_out, ffn_w13, ffn_w2))
    norm_mix, norm_ffn, hgrn_out_norm = (g[:, None, :] for g in (norm_mix, norm_ffn, hgrn_out_norm))
    bias = _bias_tiles(rel_bias_table)
    h = x
    for layer in range(depth):
        idx = layer // n_mixers
        if layer % n_mixers == 0:
            a = _hgrn_mixer(h, norm_mix, layer, hgrn_w_in, hgrn_lb_logits, hgrn_out_norm, idx)
            w_out = hgrn_w_out
        else:
            qa, ka, vt = _moba_proj(h, norm_mix, layer, moba_w_in, idx, rel_bias_table)
            a = _moba_attn(qa, ka, vt, bias)
            w_out = moba_w_out
        h = _outproj_ffn(h, a, w_out, idx, norm_ffn, ffn_w13, ffn_w2, layer, final_norm,
                         final_norm=(layer == depth - 1))
    return h
```

```python
import functools
import math

import numpy as np
import jax
import jax.numpy as jnp
from jax import lax
from jax.experimental import pallas as pl
from jax.experimental.pallas import tpu as pltpu

D_MODEL = 1024
HEADS = 8
HEAD_DIM = D_MODEL // HEADS
MOBA_BLOCK = 256
MOBA_TOPK = 3
REL_BUCKETS = 32
REL_MAX_DISTANCE = 1024
RMS_EPS = 1e-6
HGRN_CHUNK = 128
FFN_ROWS = 512
ATTN_GROUP = 4
NEG = -1e30
N_BIAS_TILES = 6
VMEM_LIMIT = 56 * 1024 * 1024
LOG2E = math.log2(math.e)
SUBLANES = 8

BF16 = jnp.bfloat16
F32 = jnp.float32


def _resident(shape):
    zeros = (0,) * len(shape)
    return pl.BlockSpec(shape, lambda *_: zeros, pipeline_mode=pl.Buffered(1))


def _layer(stacked, idx):
    zeros = (0,) * (stacked.ndim - 1)
    return pl.BlockSpec((None,) + stacked.shape[1:], lambda *_: (idx,) + zeros, pipeline_mode=pl.Buffered(1))


def _sigmoid(x):
    return 1.0 / (1.0 + jnp.exp(-x))


def _rms(x, gain):
    return x * lax.rsqrt(jnp.mean(x * x, axis=-1, keepdims=True) + RMS_EPS) * gain


def _dot(a, b):
    return jnp.dot(a, b, preferred_element_type=F32)


def _run_staggered(sequences):
    live = list(enumerate(sequences))
    step = 0
    while live:
        for i, seq in list(live):
            if step >= i and next(seq, StopIteration) is StopIteration:
                live.remove((i, seq))
        step += 1


def _hgrn_tables(c):
    ms = [c >> (i + 1) for i in range(int(math.log2(c)))]
    r = np.arange(c)
    level = np.full((c, c), -1, np.int32)
    gsel = np.zeros((len(ms), c, c), np.float32)
    sgn = np.zeros((len(ms), c, HEAD_DIM), np.float32)
    for i, m in enumerate(ms):
        upper = (r % (2 * m)) >= m
        same = (r[:, None] // (2 * m)) == (r[None, :] // (2 * m))
        level[same & upper[:, None] & ~upper[None, :]] = i
        mid = (r // (2 * m)) * 2 * m + m - 1
        gsel[i, r, mid] = 1.0
        sgn[i] = np.where(upper, 1.0, -1.0)[:, None]
    level[r, r] = len(ms)
    tri = (r[:, None] >= r[None, :]).astype(np.float32)
    fine = [i for i, m in enumerate(ms) if 2 * m < SUBLANES]
    return (jnp.asarray(tri, BF16), jnp.asarray(gsel[fine].reshape(len(fine) * c, c), BF16), jnp.asarray(sgn, F32),
            jnp.asarray(level, jnp.int32), ms)


def _hgrn_kernel(x_ref, gain_ref, w_ref, lbl_ref, onorm_ref, tri_ref, gsel_ref, sgn_ref, lvl_ref,
                 a_ref, st_ref, *, layer_idx, ms):
    bsz, c, d = x_ref.shape
    n_levels = len(ms)

    @pl.when(pl.program_id(0) == 0)
    def _():
        st_ref[...] = jnp.zeros_like(st_ref)

    logits = [lbl_ref[r:r + 1, :] for r in range(lbl_ref.shape[0])]
    top = functools.reduce(jnp.maximum, logits)
    ex = [jnp.exp(t - top) for t in logits]
    lb = sum(ex[:layer_idx + 1]) / sum(ex)

    xn = _rms(x_ref[...].reshape(bsz * c, d), gain_ref[...]).astype(BF16)
    p_all = _dot(xn, w_ref[...])

    def chunk(b):
        p = p_all[b * c:(b + 1) * c]
        qz = p[:, :d]
        q = qz * _sigmoid(qz)
        f = lb + (1.0 - lb) * _sigmoid(p[:, d:2 * d])
        k = 1.0 - f
        g = jnp.log(f) * LOG2E
        inp_b = p[:, 2 * d:3 * d].astype(BF16)
        og = _sigmoid(p[:, 3 * d:])
        g0 = g.astype(BF16)
        r1 = g - g0.astype(F32)
        g1 = r1.astype(BF16)
        g2 = (r1 - g1.astype(F32)).astype(BF16)
        yield
        tri = tri_ref[...]
        cum = _dot(tri, g0) + _dot(tri, g1) + _dot(tri, g2)
        fine_refs = _dot(gsel_ref[...], cum.astype(BF16))
        yield
        qe, ke = [], []
        for i, m in enumerate(ms):
            if 2 * m >= SUBLANES:
                blocks = cum.reshape(c // (2 * m), 2 * m, d)
                ref = jnp.broadcast_to(blocks[:, m - 1:m, :], blocks.shape).reshape(c, d)
            else:
                fi = i - (n_levels - gsel_ref.shape[0] // c)
                ref = fine_refs[fi * c:(fi + 1) * c]
            e = jnp.exp2((cum - ref) * jnp.tile(sgn_ref[i], (1, HEADS)))
            qe.append((q * e).astype(BF16))
            ke.append((k * e).T.astype(BF16))
        qe.append(q.astype(BF16))
        ke.append(k.T.astype(BF16))
        last = cum[c - 1:c, :]
        qs = (q * jnp.exp2(cum)).astype(BF16)
        kd = (k * jnp.exp2(last - cum)).T.astype(BF16)
        decay = jnp.broadcast_to(jnp.exp2(last), (SUBLANES, d)).T[:, 0:1]
        yield
        lvl = lvl_ref[...]
        outs = []
        for h in range(HEADS):
            cols = slice(h * HEAD_DIM, (h + 1) * HEAD_DIM)
            scores = jnp.zeros((c, c), F32)
            for i in range(n_levels + 1):
                scores = jnp.where(lvl == i, _dot(qe[i][:, cols], ke[i][cols, :]), scores)
            st = st_ref[b, h]
            o = _dot(scores.astype(BF16), inp_b[:, cols]) + _dot(qs[:, cols], st.astype(BF16))
            st_ref[b, h] = st * decay[cols, :] + _dot(kd[cols, :], inp_b[:, cols])
            outs.append(o)
        yield
        ss = sum(jnp.sum(o * o, axis=-1, keepdims=True) for o in outs)
        inv = lax.rsqrt(ss / D_MODEL + RMS_EPS)
        for h in range(HEADS):
            cols = slice(h * HEAD_DIM, (h + 1) * HEAD_DIM)
            a_ref[b, :, cols] = (outs[h] * inv * onorm_ref[:, cols] * og[:, cols]).astype(a_ref.dtype)

    _run_staggered([chunk(b) for b in range(bsz)])


def _hgrn_mixer(x, gains, layer, w_in, lb_logits, out_norms, layer_idx):
    bsz, seq, _ = x.shape
    c = HGRN_CHUNK
    tri, gsel, sgn, lvl, ms = _hgrn_tables(c)
    kern = functools.partial(_hgrn_kernel, layer_idx=layer_idx, ms=tuple(ms))
    return pl.pallas_call(
        kern,
        grid=(seq // c,),
        in_specs=[
            pl.BlockSpec((bsz, c, D_MODEL), lambda i: (0, i, 0)),
            _layer(gains, layer),
            _layer(w_in, layer_idx),
            _resident(lb_logits.shape),
            _layer(out_norms, layer_idx),
            _resident(tri.shape), _resident(gsel.shape), _resident(sgn.shape), _resident(lvl.shape),
        ],
        out_specs=pl.BlockSpec((bsz, c, D_MODEL), lambda i: (0, i, 0)),
        out_shape=jax.ShapeDtypeStruct((bsz, seq, D_MODEL), BF16),
        scratch_shapes=[pltpu.VMEM((bsz, HEADS, HEAD_DIM, HEAD_DIM), F32)],
        compiler_params=pltpu.CompilerParams(
            dimension_semantics=("arbitrary",), vmem_limit_bytes=VMEM_LIMIT),
        name="hgrn_mixer",
    )(x, gains, w_in, lb_logits, out_norms, tri, gsel, sgn, lvl)


def _ffn_kernel(res_ref, a_ref, wo_ref, gain_ref, w13_ref, w2_ref, fin_ref, out_ref, *, final_norm):
    d_ff = w2_ref.shape[0]
    h = res_ref[...] + _dot(a_ref[...], wo_ref[...])
    xn = _rms(h, gain_ref[...]).astype(BF16)
    gu = _dot(xn, w13_ref[...])
    g = gu[:, :d_ff]
    u = gu[:, d_ff:]
    act = (g * _sigmoid(g) * u).astype(BF16)
    h = h + _dot(act, w2_ref[...])
    if final_norm:
        h = _rms(h, fin_ref[...])
    out_ref[...] = h


def _outproj_ffn(res, a, w_out, idx, gains, w13, w2, layer, final_gain, final_norm):
    bsz, seq, _ = res.shape
    rows = bsz * seq
    kern = functools.partial(_ffn_kernel, final_norm=final_norm)
    row_spec = pl.BlockSpec((FFN_ROWS, D_MODEL), lambda i: (i, 0))
    out = pl.pallas_call(
        kern,
        grid=(rows // FFN_ROWS,),
        in_specs=[
            row_spec, row_spec,
            _layer(w_out, idx),
            _layer(gains, layer),
            _layer(w13, layer),
            _layer(w2, layer),
            _resident((1, D_MODEL)),
        ],
        out_specs=row_spec,
        out_shape=jax.ShapeDtypeStruct((rows, D_MODEL), F32),
        compiler_params=pltpu.CompilerParams(
            dimension_semantics=("arbitrary",), vmem_limit_bytes=VMEM_LIMIT),
        name="outproj_ffn_final" if final_norm else "outproj_ffn",
    )(res.reshape(rows, D_MODEL), a.reshape(rows, D_MODEL), w_out, gains, w13, w2, final_gain.reshape(1, D_MODEL))
    return out.reshape(bsz, seq, D_MODEL)


def _t5_thresholds():
    max_exact = REL_BUCKETS // 2
    n_log = REL_BUCKETS - max_exact
    ratio = REL_MAX_DISTANCE / max_exact
    return [int(math.ceil(max_exact * ratio ** (k / n_log) - 1e-9)) for k in range(1, n_log)]


def _t5_bucket(dist):
    max_exact = REL_BUCKETS // 2
    return dist if dist < max_exact else max_exact + sum(dist >= t for t in _t5_thresholds())


def _bias_kernel(tab_ref, out_ref):
    dl = pl.program_id(0)
    tk = lax.broadcasted_iota(jnp.int32, (MOBA_BLOCK, MOBA_BLOCK), 0)
    tq = lax.broadcasted_iota(jnp.int32, (MOBA_BLOCK, MOBA_BLOCK), 1)
    max_exact = REL_BUCKETS // 2
    for tile in range(N_BIAS_TILES - 1):
        @pl.when(dl == tile)
        def _():
            dist = tile * MOBA_BLOCK + tq - tk
            d_lo = max(tile * MOBA_BLOCK - (MOBA_BLOCK - 1), 0)
            d_hi = tile * MOBA_BLOCK + MOBA_BLOCK - 1
            b_lo, b_hi = _t5_bucket(d_lo), _t5_bucket(d_hi)
            large = jnp.full(dist.shape, max_exact, jnp.int32)
            for t in _t5_thresholds():
                if t <= d_hi:
                    large = large + (dist >= t).astype(jnp.int32)
            bucket = jnp.where(dist < max_exact, dist, large) if d_lo < max_exact else large
            for h in range(HEADS):
                bias = jnp.full(dist.shape, tab_ref[b_lo, h] * LOG2E, F32)
                for bkt in range(b_lo + 1, b_hi + 1):
                    bias = jnp.where(bucket == bkt, tab_ref[bkt, h] * LOG2E, bias)
                if tile == 0:
                    bias = jnp.where(dist < 0, NEG, bias)
                out_ref[0, h] = bias

    @pl.when(dl == N_BIAS_TILES - 1)
    def _():
        out_ref[...] = jnp.zeros_like(out_ref)


def _bias_tiles(rel_table):
    return pl.pallas_call(
        _bias_kernel,
        grid=(N_BIAS_TILES,),
        in_specs=[pl.BlockSpec(memory_space=pltpu.SMEM)],
        out_specs=pl.BlockSpec((1, HEADS, MOBA_BLOCK, MOBA_BLOCK), lambda i: (i, 0, 0, 0)),
        out_shape=jax.ShapeDtypeStruct((N_BIAS_TILES, HEADS, MOBA_BLOCK, MOBA_BLOCK), F32),
        compiler_params=pltpu.CompilerParams(dimension_semantics=("arbitrary",)),
        name="moba_bias_tiles",
    )(rel_table)


def _moba_proj_kernel(x_ref, gain_ref, w_ref, tab_ref, qa_ref, ka_ref, vt_ref, kmean_scr):
    j = pl.program_id(0)
    bsz, nb = kmean_scr.shape[0], kmean_scr.shape[2]
    width = 2 * HEAD_DIM
    col_head = lax.broadcasted_iota(jnp.int32, (1, D_MODEL), 1) // HEAD_DIM

    @pl.when(j == 0)
    def _():
        kmean_scr[...] = jnp.zeros_like(kmean_scr)

    xn = _rms(x_ref[...].reshape(bsz * MOBA_BLOCK, D_MODEL), gain_ref[...]).astype(BF16)
    qkv = _dot(xn, w_ref[...])

    lane = lax.broadcasted_iota(jnp.int32, (MOBA_BLOCK, HEAD_DIM), 1)
    onehot = ((lane == j) | (lane == j + nb)).astype(BF16)
    blk = lax.broadcasted_iota(jnp.int32, (HEADS, nb, MOBA_BLOCK), 1)
    blk_f = blk.astype(F32)
    far_bias = jnp.concatenate(
        [jnp.broadcast_to(tab_ref[REL_BUCKETS - 1:REL_BUCKETS, h:h + 1] * LOG2E, (1, nb, MOBA_BLOCK)) for h in range(HEADS)],
        axis=0)
    far_bias = jnp.where(j - blk >= N_BIAS_TILES - 1, far_bias, 0.0)
    pad = jnp.zeros((HEAD_DIM - 2 * nb, MOBA_BLOCK), F32)
    for b in range(bsz):
        rows = slice(b * MOBA_BLOCK, (b + 1) * MOBA_BLOCK)
        q = qkv[rows, :D_MODEL]
        k = qkv[rows, D_MODEL:2 * D_MODEL]
        v = qkv[rows, 2 * D_MODEL:]
        vt_ref[b, :, 0] = v.T.reshape(HEADS, HEAD_DIM, MOBA_BLOCK).astype(BF16)
        qt_all = q.T
        km = kmean_scr[b].reshape(HEADS * nb, D_MODEL)
        km_hi = km.astype(BF16)
        km_lo = (km - km_hi.astype(F32)).astype(BF16)
        qt_hi = qt_all.astype(BF16)
        qt_lo = (qt_all - qt_hi.astype(F32)).astype(BF16)
        gate = (_dot(km_hi, qt_hi) + _dot(km_hi, qt_lo) + _dot(km_lo, qt_hi)).reshape(HEADS, nb, MOBA_BLOCK)
        gate = jnp.where(blk < j, gate, -jnp.inf)
        madd = jnp.full(gate.shape, NEG, F32)
        for _ in range(MOBA_TOPK):
            top = jnp.max(gate, axis=1, keepdims=True)
            idx = jnp.min(jnp.where(gate == top, blk_f, float(nb)), axis=1, keepdims=True)
            hit = blk_f == idx
            madd = jnp.where(hit & (top > -jnp.inf), far_bias, madd)
            gate = jnp.where(hit, -jnp.inf, gate)
        madd = jnp.where(blk == j, 0.0, madd)
        hi = madd.astype(BF16).astype(F32)
        lo = (madd - hi).astype(BF16).astype(F32)
        for h in range(HEADS):
            cols = slice(h * HEAD_DIM, (h + 1) * HEAD_DIM)
            qt = qt_all[cols, :] * (LOG2E * HEAD_DIM ** -0.5)
            qa_ref[b, h, 0] = jnp.concatenate([qt, hi[h], lo[h], pad], axis=0).astype(BF16)
            ka_ref[b, :, h * width:h * width + HEAD_DIM] = k[:, cols].astype(BF16)
            ka_ref[b, :, h * width + HEAD_DIM:(h + 1) * width] = onehot
        kmean = jnp.mean(k, axis=0, keepdims=True)
        for h in range(HEADS):
            kmean_scr[b, h, pl.ds(j, 1), :] = jnp.where(col_head == h, kmean, 0.0)


def _moba_proj(h, gains, layer, w_in, idx, rel_table):
    bsz, seq, _ = h.shape
    nb = seq // MOBA_BLOCK
    assert 2 * nb <= HEAD_DIM
    return pl.pallas_call(
        _moba_proj_kernel,
        grid=(nb,),
        in_specs=[
            pl.BlockSpec((bsz, MOBA_BLOCK, D_MODEL), lambda j: (0, j, 0)),
            _layer(gains, layer),
            _layer(w_in, idx),
            _resident(rel_table.shape),
        ],
        out_specs=[
            pl.BlockSpec((bsz, HEADS, 1, 2 * HEAD_DIM, MOBA_BLOCK), lambda j: (0, 0, j, 0, 0)),
            pl.BlockSpec((bsz, MOBA_BLOCK, 2 * D_MODEL), lambda j: (0, j, 0)),
            pl.BlockSpec((bsz, HEADS, 1, HEAD_DIM, MOBA_BLOCK), lambda j: (0, 0, j, 0, 0)),
        ],
        out_shape=[
            jax.ShapeDtypeStruct((bsz, HEADS, nb, 2 * HEAD_DIM, MOBA_BLOCK), BF16),
            jax.ShapeDtypeStruct((bsz, seq, 2 * D_MODEL), BF16),
            jax.ShapeDtypeStruct((bsz, HEADS, nb, HEAD_DIM, MOBA_BLOCK), BF16),
        ],
        scratch_shapes=[pltpu.VMEM((bsz, HEADS, nb, D_MODEL), F32)],
        compiler_params=pltpu.CompilerParams(
            dimension_semantics=("arbitrary",), vmem_limit_bytes=VMEM_LIMIT),
        name="moba_proj",
    )(h, gains, w_in, rel_table)


def _attn_tile(qa_ref, ka_ref, vt_ref, bias_ref, out_ref, s_scr, b, t, j, n_groups):
    qa = qa_ref[b, 0, t]
    fold = (MOBA_BLOCK // SUBLANES, SUBLANES, MOBA_BLOCK)
    first_near = (n_groups - 1) * ATTN_GROUP - (N_BIAS_TILES - 2)

    def qk_group(g):
        gmax = None
        for u in range(ATTN_GROUP):
            n = g * ATTN_GROUP + u
            s = _dot(ka_ref[b, n * MOBA_BLOCK:(n + 1) * MOBA_BLOCK, :], qa)
            if n >= first_near:
                s = s + bias_ref[jnp.clip(j - n, 0, N_BIAS_TILES - 1), 0]
            s_scr[b, g % 2, u] = s
            tmax = jnp.max(s.reshape(fold), axis=0)
            gmax = tmax if gmax is None else jnp.maximum(gmax, tmax)
        return gmax

    m = jnp.full((1, MOBA_BLOCK), NEG, F32)
    lrun = jnp.zeros((SUBLANES, MOBA_BLOCK), F32)
    acc = jnp.zeros((HEAD_DIM, MOBA_BLOCK), F32)
    gmax = qk_group(0)
    yield
    for g in range(n_groups):
        nxt = qk_group(g + 1) if g + 1 < n_groups else None
        yield
        m_new = jnp.maximum(m, jnp.max(gmax, axis=0, keepdims=True))
        alpha = jnp.exp2(m - m_new)
        lsum = jnp.zeros((SUBLANES, MOBA_BLOCK), F32)
        pv = jnp.zeros((HEAD_DIM, MOBA_BLOCK), F32)
        for u in range(ATTN_GROUP):
            p = jnp.exp2(s_scr[b, g % 2, u] - m_new)
            lsum = lsum + jnp.sum(p.reshape(fold), axis=0)
            pv = pv + _dot(vt_ref[b, 0, g * ATTN_GROUP + u], p.astype(BF16))
        lrun = alpha * lrun + lsum
        acc = alpha * acc + pv
        m = m_new
        gmax = nxt
        yield
    l = jnp.sum(lrun, axis=0, keepdims=True)
    out_ref[b, pl.ds(pl.multiple_of(t * MOBA_BLOCK, MOBA_BLOCK), MOBA_BLOCK), :] = (acc / l).T.astype(out_ref.dtype)


def _moba_attn_kernel(qa_ref, ka_ref, vt_ref, bias_ref, out_ref, s_scr):
    c = pl.program_id(1)
    bsz = qa_ref.shape[0]
    nb = vt_ref.shape[2]
    for n_groups in range(1, nb // ATTN_GROUP + 1):
        @pl.when(c == n_groups - 1)
        def _():
            def tile(t, carry):
                j = c * ATTN_GROUP + t
                _run_staggered([_attn_tile(qa_ref, ka_ref, vt_ref, bias_ref, out_ref, s_scr, b, t, j, n_groups)
                                for b in range(bsz)])
                return carry
            lax.fori_loop(0, ATTN_GROUP, tile, 0)


def _moba_attn(qa, ka, vt, bias):
    bsz, seq, _ = ka.shape
    nb = seq // MOBA_BLOCK
    assert nb % ATTN_GROUP == 0
    width = 2 * HEAD_DIM
    return pl.pallas_call(
        _moba_attn_kernel,
        grid=(HEADS, nb // ATTN_GROUP),
        in_specs=[
            pl.BlockSpec((bsz, 1, ATTN_GROUP, width, MOBA_BLOCK), lambda h, c: (0, h, c, 0, 0)),
            pl.BlockSpec((bsz, seq, width), lambda h, c: (0, 0, h)),
            pl.BlockSpec((bsz, 1, nb, HEAD_DIM, MOBA_BLOCK), lambda h, c: (0, h, 0, 0, 0)),
            pl.BlockSpec((N_BIAS_TILES, 1, MOBA_BLOCK, MOBA_BLOCK), lambda h, c: (0, h, 0, 0)),
        ],
        out_specs=pl.BlockSpec((bsz, ATTN_GROUP * MOBA_BLOCK, HEAD_DIM), lambda h, c: (0, c, h)),
        out_shape=jax.ShapeDtypeStruct((bsz, seq, D_MODEL), BF16),
        scratch_shapes=[pltpu.VMEM((bsz, 2, ATTN_GROUP, MOBA_BLOCK, MOBA_BLOCK), F32)],
        compiler_params=pltpu.CompilerParams(
            dimension_semantics=("arbitrary", "arbitrary"), vmem_limit_bytes=VMEM_LIMIT),
        name="moba_attn",
    )(qa, ka, vt, bias)


def kernel(x, norm_mix, norm_ffn, hgrn_w_in, hgrn_lb_logits, hgrn_out_norm, hgrn_w_out,
           moba_w_in, moba_w_out, rel_bias_table, ffn_w13, ffn_w2, final_norm):
    depth = norm_mix.shape[0]
    n_mixers = 2
    hgrn_w_in, hgrn_w_out, moba_w_in, moba_w_out, ffn_w13, ffn_w2 = (
        w.astype(BF16) for w in (hgrn_w_in, hgrn_w_out, moba_w_in, moba_w_out, ffn_w13, ffn_w2))
    norm_mix, norm_ffn, hgrn_out_norm = (g[:, None, :] for g in (norm_mix, norm_ffn, hgrn_out_norm))
    bias = _bias_tiles(rel_bias_table)
    h = x
    for layer in range(depth):
        idx = layer // n_mixers
        if layer % n_mixers == 0:
            a = _hgrn_mixer(h, norm_mix, layer, hgrn_w_in, hgrn_lb_logits, hgrn_out_norm, idx)
            w_out = hgrn_w_out
        else:
            qa, ka, vt = _moba_proj(h, norm_mix, layer, moba_w_in, idx, rel_bias_table)
            a = _moba_attn(qa, ka, vt, bias)
            w_out = moba_w_out
        h = _outproj_ffn(h, a, w_out, idx, norm_ffn, ffn_w13, ffn_w2, layer, final_norm,
                         final_norm=(layer == depth - 1))
    return h
```

```python
import functools
import math

import numpy as np
import jax
import jax.numpy as jnp
from jax import lax
from jax.experimental import pallas as pl
from jax.experimental.pallas import tpu as pltpu

D_MODEL = 1024
HEADS = 8
HEAD_DIM = D_MODEL // HEADS
MOBA_BLOCK = 256
MOBA_TOPK = 3
REL_BUCKETS = 32
REL_MAX_DISTANCE = 1024
RMS_EPS = 1e-6
HGRN_CHUNK = 128
HGRN_CHUNKS_PER_STEP = 4
FFN_ROWS = 512
ATTN_GROUP = 4
NEG = -1e30
N_BIAS_TILES = 6
VMEM_LIMIT = 56 * 1024 * 1024
LOG2E = math.log2(math.e)
SUBLANES = 8

BF16 = jnp.bfloat16
F32 = jnp.float32


def _resident(shape):
    zeros = (0,) * len(shape)
    return pl.BlockSpec(shape, lambda *_: zeros, pipeline_mode=pl.Buffered(1))


def _layer(stacked, idx):
    zeros = (0,) * (stacked.ndim - 1)
    return pl.BlockSpec((None,) + stacked.shape[1:], lambda *_: (idx,) + zeros, pipeline_mode=pl.Buffered(1))


def _sigmoid(x):
    return 1.0 / (1.0 + jnp.exp(-x))


def _rms(x, gain):
    return x * lax.rsqrt(jnp.mean(x * x, axis=-1, keepdims=True) + RMS_EPS) * gain


def _dot(a, b):
    return jnp.dot(a, b, preferred_element_type=F32)


def _run_staggered(sequences):
    live = list(enumerate(sequences))
    step = 0
    while live:
        for i, seq in list(live):
            if step >= i and next(seq, StopIteration) is StopIteration:
                live.remove((i, seq))
        step += 1


def _hgrn_tables(c):
    ms = [c >> (i + 1) for i in range(int(math.log2(c)))]
    r = np.arange(c)
    level = np.full((c, c), -1, np.int32)
    gsel = np.zeros((len(ms), c, c), np.float32)
    sgn = np.zeros((len(ms), c, HEAD_DIM), np.float32)
    for i, m in enumerate(ms):
        upper = (r % (2 * m)) >= m
        same = (r[:, None] // (2 * m)) == (r[None, :] // (2 * m))
        level[same & upper[:, None] & ~upper[None, :]] = i
        mid = (r // (2 * m)) * 2 * m + m - 1
        gsel[i, r, mid] = 1.0
        sgn[i] = np.where(upper, 1.0, -1.0)[:, None]
    level[r, r] = len(ms)
    tri = (r[:, None] >= r[None, :]).astype(np.float32)
    fine = [i for i, m in enumerate(ms) if 2 * m < SUBLANES]
    return (jnp.asarray(tri, BF16), jnp.asarray(gsel[fine].reshape(len(fine) * c, c), BF16), jnp.asarray(sgn, F32),
            jnp.asarray(level, jnp.int32), ms)


def _hgrn_kernel(x_ref, gain_ref, w_ref, lbl_ref, onorm_ref, tri_ref, gsel_ref, sgn_ref, lvl_ref,
                 a_ref, st_ref, *, layer_idx, ms):
    bsz, _, d = x_ref.shape
    c = HGRN_CHUNK
    n_levels = len(ms)

    @pl.when(pl.program_id(0) == 0)
    def _():
        st_ref[...] = jnp.zeros_like(st_ref)

    logits = [lbl_ref[r:r + 1, :] for r in range(lbl_ref.shape[0])]
    top = functools.reduce(jnp.maximum, logits)
    ex = [jnp.exp(t - top) for t in logits]
    lb = sum(ex[:layer_idx + 1]) / sum(ex)

    def chunk(b, p_all, rows):
        p = p_all[b * c:(b + 1) * c]
        qz = p[:, :d]
        q = qz * _sigmoid(qz)
        f = lb + (1.0 - lb) * _sigmoid(p[:, d:2 * d])
        k = 1.0 - f
        g = jnp.log(f) * LOG2E
        inp_b = p[:, 2 * d:3 * d].astype(BF16)
        og = _sigmoid(p[:, 3 * d:])
        g0 = g.astype(BF16)
        r1 = g - g0.astype(F32)
        g1 = r1.astype(BF16)
        g2 = (r1 - g1.astype(F32)).astype(BF16)
        yield
        tri = tri_ref[...]
        cum = _dot(tri, g0) + _dot(tri, g1) + _dot(tri, g2)
        fine_refs = _dot(gsel_ref[...], cum.astype(BF16))
        yield
        qe, ke = [], []
        for i, m in enumerate(ms):
            if 2 * m >= SUBLANES:
                blocks = cum.reshape(c // (2 * m), 2 * m, d)
                ref = jnp.broadcast_to(blocks[:, m - 1:m, :], blocks.shape).reshape(c, d)
            else:
                fi = i - (n_levels - gsel_ref.shape[0] // c)
                ref = fine_refs[fi * c:(fi + 1) * c]
            e = jnp.exp2((cum - ref) * jnp.tile(sgn_ref[i], (1, HEADS)))
            qe.append((q * e).astype(BF16))
            ke.append((k * e).T.astype(BF16))
        qe.append(q.astype(BF16))
        ke.append(k.T.astype(BF16))
        last = cum[c - 1:c, :]
        qs = (q * jnp.exp2(cum)).astype(BF16)
        kd = (k * jnp.exp2(last - cum)).T.astype(BF16)
        decay = jnp.broadcast_to(jnp.exp2(last), (SUBLANES, d)).T[:, 0:1]
        yield
        lvl = lvl_ref[...]
        outs = []
        for h in range(HEADS):
            cols = slice(h * HEAD_DIM, (h + 1) * HEAD_DIM)
            scores = jnp.zeros((c, c), F32)
            for i in range(n_levels + 1):
                scores = jnp.where(lvl == i, _dot(qe[i][:, cols], ke[i][cols, :]), scores)
            st = st_ref[b, h]
            o = _dot(scores.astype(BF16), inp_b[:, cols]) + _dot(qs[:, cols], st.astype(BF16))
            st_ref[b, h] = st * decay[cols, :] + _dot(kd[cols, :], inp_b[:, cols])
            outs.append(o)
        yield
        ss = sum(jnp.sum(o * o, axis=-1, keepdims=True) for o in outs)
        inv = lax.rsqrt(ss / D_MODEL + RMS_EPS)
        for h in range(HEADS):
            cols = slice(h * HEAD_DIM, (h + 1) * HEAD_DIM)
            a_ref[b, rows, cols] = (outs[h] * inv * onorm_ref[:, cols] * og[:, cols]).astype(a_ref.dtype)

    def step(t, carry):
        rows = pl.ds(pl.multiple_of(t * c, c), c)
        xn = _rms(x_ref[:, rows, :].reshape(bsz * c, d), gain_ref[...]).astype(BF16)
        p_all = _dot(xn, w_ref[...])
        _run_staggered([chunk(b, p_all, rows) for b in range(bsz)])
        return carry

    lax.fori_loop(0, x_ref.shape[1] // c, step, 0)


def _hgrn_mixer(x, gains, layer, w_in, lb_logits, out_norms, layer_idx):
    bsz, seq, _ = x.shape
    c = HGRN_CHUNK
    rows_per_step = c * HGRN_CHUNKS_PER_STEP
    tri, gsel, sgn, lvl, ms = _hgrn_tables(c)
    kern = functools.partial(_hgrn_kernel, layer_idx=layer_idx, ms=tuple(ms))
    return pl.pallas_call(
        kern,
        grid=(seq // rows_per_step,),
        in_specs=[
            pl.BlockSpec((bsz, rows_per_step, D_MODEL), lambda i: (0, i, 0)),
            _layer(gains, layer),
            _layer(w_in, layer_idx),
            _resident(lb_logits.shape),
            _layer(out_norms, layer_idx),
            _resident(tri.shape), _resident(gsel.shape), _resident(sgn.shape), _resident(lvl.shape),
        ],
        out_specs=pl.BlockSpec((bsz, rows_per_step, D_MODEL), lambda i: (0, i, 0)),
        out_shape=jax.ShapeDtypeStruct((bsz, seq, D_MODEL), BF16),
        scratch_shapes=[pltpu.VMEM((bsz, HEADS, HEAD_DIM, HEAD_DIM), F32)],
        compiler_params=pltpu.CompilerParams(
            dimension_semantics=("arbitrary",), vmem_limit_bytes=VMEM_LIMIT),
        name="hgrn_mixer",
    )(x, gains, w_in, lb_logits, out_norms, tri, gsel, sgn, lvl)


def _ffn_kernel(res_ref, a_ref, wo_ref, gain_ref, w13_ref, w2_ref, fin_ref, out_ref, *, final_norm):
    d_ff = w2_ref.shape[0]
    h = res_ref[...] + _dot(a_ref[...], wo_ref[...])
    xn = _rms(h, gain_ref[...]).astype(BF16)
    gu = _dot(xn, w13_ref[...])
    g = gu[:, :d_ff]
    u = gu[:, d_ff:]
    act = (g * _sigmoid(g) * u).astype(BF16)
    h = h + _dot(act, w2_ref[...])
    if final_norm:
        h = _rms(h, fin_ref[...])
    out_ref[...] = h


def _outproj_ffn(res, a, w_out, idx, gains, w13, w2, layer, final_gain, final_norm):
    bsz, seq, _ = res.shape
    rows = bsz * seq
    kern = functools.partial(_ffn_kernel, final_norm=final_norm)
    row_spec = pl.BlockSpec((FFN_ROWS, D_MODEL), lambda i: (i, 0))
    out = pl.pallas_call(
        kern,
        grid=(rows // FFN_ROWS,),
        in_specs=[
            row_spec, row_spec,
            _layer(w_out, idx),
            _layer(gains, layer),
            _layer(w13, layer),
            _layer(w2, layer),
            _resident((1, D_MODEL)),
        ],
        out_specs=row_spec,
        out_shape=jax.ShapeDtypeStruct((rows, D_MODEL), F32),
        compiler_params=pltpu.CompilerParams(
            dimension_semantics=("arbitrary",), vmem_limit_bytes=VMEM_LIMIT),
        name="outproj_ffn_final" if final_norm else "outproj_ffn",
    )(res.reshape(rows, D_MODEL), a.reshape(rows, D_MODEL), w_out, gains, w13, w2, final_gain.reshape(1, D_MODEL))
    return out.reshape(bsz, seq, D_MODEL)


def _t5_thresholds():
    max_exact = REL_BUCKETS // 2
    n_log = REL_BUCKETS - max_exact
    ratio = REL_MAX_DISTANCE / max_exact
    return [int(math.ceil(max_exact * ratio ** (k / n_log) - 1e-9)) for k in range(1, n_log)]


def _t5_bucket(dist):
    max_exact = REL_BUCKETS // 2
    return dist if dist < max_exact else max_exact + sum(dist >= t for t in _t5_thresholds())


def _bias_kernel(tab_ref, out_ref):
    dl = pl.program_id(0)
    tk = lax.broadcasted_iota(jnp.int32, (MOBA_BLOCK, MOBA_BLOCK), 0)
    tq = lax.broadcasted_iota(jnp.int32, (MOBA_BLOCK, MOBA_BLOCK), 1)
    max_exact = REL_BUCKETS // 2
    for tile in range(N_BIAS_TILES - 1):
        @pl.when(dl == tile)
        def _():
            dist = tile * MOBA_BLOCK + tq - tk
            d_lo = max(tile * MOBA_BLOCK - (MOBA_BLOCK - 1), 0)
            d_hi = tile * MOBA_BLOCK + MOBA_BLOCK - 1
            b_lo, b_hi = _t5_bucket(d_lo), _t5_bucket(d_hi)
            large = jnp.full(dist.shape, max_exact, jnp.int32)
            for t in _t5_thresholds():
                if t <= d_hi:
                    large = large + (dist >= t).astype(jnp.int32)
            bucket = jnp.where(dist < max_exact, dist, large) if d_lo < max_exact else large
            for h in range(HEADS):
                bias = jnp.full(dist.shape, tab_ref[b_lo, h] * LOG2E, F32)
                for bkt in range(b_lo + 1, b_hi + 1):
                    bias = jnp.where(bucket == bkt, tab_ref[bkt, h] * LOG2E, bias)
                if tile == 0:
                    bias = jnp.where(dist < 0, NEG, bias)
                out_ref[0, h] = bias

    @pl.when(dl == N_BIAS_TILES - 1)
    def _():
        out_ref[...] = jnp.zeros_like(out_ref)


def _bias_tiles(rel_table):
    return pl.pallas_call(
        _bias_kernel,
        grid=(N_BIAS_TILES,),
        in_specs=[pl.BlockSpec(memory_space=pltpu.SMEM)],
        out_specs=pl.BlockSpec((1, HEADS, MOBA_BLOCK, MOBA_BLOCK), lambda i: (i, 0, 0, 0)),
        out_shape=jax.ShapeDtypeStruct((N_BIAS_TILES, HEADS, MOBA_BLOCK, MOBA_BLOCK), F32),
        compiler_params=pltpu.CompilerParams(dimension_semantics=("arbitrary",)),
        name="moba_bias_tiles",
    )(rel_table)


def _moba_proj_kernel(x_ref, gain_ref, w_ref, tab_ref, qa_ref, ka_ref, vt_ref, kmean_scr):
    j = pl.program_id(0)
    bsz, nb = kmean_scr.shape[0], kmean_scr.shape[2]
    width = 2 * HEAD_DIM
    col_head = lax.broadcasted_iota(jnp.int32, (1, D_MODEL), 1) // HEAD_DIM

    @pl.when(j == 0)
    def _():
        kmean_scr[...] = jnp.zeros_like(kmean_scr)

    xn = _rms(x_ref[...].reshape(bsz * MOBA_BLOCK, D_MODEL), gain_ref[...]).astype(BF16)
    qkv = _dot(xn, w_ref[...])

    lane = lax.broadcasted_iota(jnp.int32, (MOBA_BLOCK, HEAD_DIM), 1)
    onehot = ((lane == j) | (lane == j + nb)).astype(BF16)
    blk = lax.broadcasted_iota(jnp.int32, (HEADS, nb, MOBA_BLOCK), 1)
    blk_f = blk.astype(F32)
    far_bias = jnp.concatenate(
        [jnp.broadcast_to(tab_ref[REL_BUCKETS - 1:REL_BUCKETS, h:h + 1] * LOG2E, (1, nb, MOBA_BLOCK)) for h in range(HEADS)],
        axis=0)
    far_bias = jnp.where(j - blk >= N_BIAS_TILES - 1, far_bias, 0.0)
    pad = jnp.zeros((HEAD_DIM - 2 * nb, MOBA_BLOCK), F32)
    for b in range(bsz):
        rows = slice(b * MOBA_BLOCK, (b + 1) * MOBA_BLOCK)
        q = qkv[rows, :D_MODEL]
        k = qkv[rows, D_MODEL:2 * D_MODEL]
        v = qkv[rows, 2 * D_MODEL:]
        vt_ref[b, :, 0] = v.T.reshape(HEADS, HEAD_DIM, MOBA_BLOCK).astype(BF16)
        qt_all = q.T
        km = kmean_scr[b].reshape(HEADS * nb, D_MODEL)
        km_hi = km.astype(BF16)
        km_lo = (km - km_hi.astype(F32)).astype(BF16)
        qt_hi = qt_all.astype(BF16)
        qt_lo = (qt_all - qt_hi.astype(F32)).astype(BF16)
        gate = (_dot(km_hi, qt_hi) + _dot(km_hi, qt_lo) + _dot(km_lo, qt_hi)).reshape(HEADS, nb, MOBA_BLOCK)
        gate = jnp.where(blk < j, gate, -jnp.inf)
        madd = jnp.full(gate.shape, NEG, F32)
        for _ in range(MOBA_TOPK):
            top = jnp.max(gate, axis=1, keepdims=True)
            idx = jnp.min(jnp.where(gate == top, blk_f, float(nb)), axis=1, keepdims=True)
            hit = blk_f == idx
            madd = jnp.where(hit & (top > -jnp.inf), far_bias, madd)
            gate = jnp.where(hit, -jnp.inf, gate)
        madd = jnp.where(blk == j, 0.0, madd)
        hi = madd.astype(BF16).astype(F32)
        lo = (madd - hi).astype(BF16).astype(F32)
        for h in range(HEADS):
            cols = slice(h * HEAD_DIM, (h + 1) * HEAD_DIM)
            qt = qt_all[cols, :] * (LOG2E * HEAD_DIM ** -0.5)
            qa_ref[b, h, 0] = jnp.concatenate([qt, hi[h], lo[h], pad], axis=0).astype(BF16)
            ka_ref[b, :, h * width:h * width + HEAD_DIM] = k[:, cols].astype(BF16)
            ka_ref[b, :, h * width + HEAD_DIM:(h + 1) * width] = onehot
        kmean = jnp.mean(k, axis=0, keepdims=True)
        for h in range(HEADS):
            kmean_scr[b, h, pl.ds(j, 1), :] = jnp.where(col_head == h, kmean, 0.0)


def _moba_proj(h, gains, layer, w_in, idx, rel_table):
    bsz, seq, _ = h.shape
    nb = seq // MOBA_BLOCK
    assert 2 * nb <= HEAD_DIM
    return pl.pallas_call(
        _moba_proj_kernel,
        grid=(nb,),
        in_specs=[
            pl.BlockSpec((bsz, MOBA_BLOCK, D_MODEL), lambda j: (0, j, 0)),
            _layer(gains, layer),
            _layer(w_in, idx),
            _resident(rel_table.shape),
        ],
        out_specs=[
            pl.BlockSpec((bsz, HEADS, 1, 2 * HEAD_DIM, MOBA_BLOCK), lambda j: (0, 0, j, 0, 0)),
            pl.BlockSpec((bsz, MOBA_BLOCK, 2 * D_MODEL), lambda j: (0, j, 0)),
            pl.BlockSpec((bsz, HEADS, 1, HEAD_DIM, MOBA_BLOCK), lambda j: (0, 0, j, 0, 0)),
        ],
        out_shape=[
            jax.ShapeDtypeStruct((bsz, HEADS, nb, 2 * HEAD_DIM, MOBA_BLOCK), BF16),
            jax.ShapeDtypeStruct((bsz, seq, 2 * D_MODEL), BF16),
            jax.ShapeDtypeStruct((bsz, HEADS, nb, HEAD_DIM, MOBA_BLOCK), BF16),
        ],
        scratch_shapes=[pltpu.VMEM((bsz, HEADS, nb, D_MODEL), F32)],
        compiler_params=pltpu.CompilerParams(
            dimension_semantics=("arbitrary",), vmem_limit_bytes=VMEM_LIMIT),
        name="moba_proj",
    )(h, gains, w_in, rel_table)


def _attn_tile(qa_ref, ka_ref, vt_ref, bias_ref, out_ref, s_scr, b, t, j, n_groups):
    qa = qa_ref[b, 0, t]
    fold = (MOBA_BLOCK // SUBLANES, SUBLANES, MOBA_BLOCK)
    first_near = (n_groups - 1) * ATTN_GROUP - (N_BIAS_TILES - 2)

    def qk_group(g):
        gmax = None
        for u in range(ATTN_GROUP):
            n = g * ATTN_GROUP + u
            s = _dot(ka_ref[b, n * MOBA_BLOCK:(n + 1) * MOBA_BLOCK, :], qa)
            if n >= first_near:
                s = s + bias_ref[jnp.clip(j - n, 0, N_BIAS_TILES - 1), 0]
            s_scr[b, g % 2, u] = s
            tmax = jnp.max(s.reshape(fold), axis=0)
            gmax = tmax if gmax is None else jnp.maximum(gmax, tmax)
        return gmax

    m = jnp.full((1, MOBA_BLOCK), NEG, F32)
    lrun = jnp.zeros((SUBLANES, MOBA_BLOCK), F32)
    acc = jnp.zeros((HEAD_DIM, MOBA_BLOCK), F32)
    gmax = qk_group(0)
    yield
    for g in range(n_groups):
        nxt = qk_group(g + 1) if g + 1 < n_groups else None
        yield
        m_new = jnp.maximum(m, jnp.max(gmax, axis=0, keepdims=True))
        alpha = jnp.exp2(m - m_new)
        lsum = jnp.zeros((SUBLANES, MOBA_BLOCK), F32)
        pv = jnp.zeros((HEAD_DIM, MOBA_BLOCK), F32)
        for u in range(ATTN_GROUP):
            p = jnp.exp2(s_scr[b, g % 2, u] - m_new)
            lsum = lsum + jnp.sum(p.reshape(fold), axis=0)
            pv = pv + _dot(vt_ref[b, 0, g * ATTN_GROUP + u], p.astype(BF16))
        lrun = alpha * lrun + lsum
        acc = alpha * acc + pv
        m = m_new
        gmax = nxt
        yield
    l = jnp.sum(lrun, axis=0, keepdims=True)
    out_ref[b, pl.ds(pl.multiple_of(t * MOBA_BLOCK, MOBA_BLOCK), MOBA_BLOCK), :] = (acc / l).T.astype(out_ref.dtype)


def _moba_attn_kernel(qa_ref, ka_ref, vt_ref, bias_ref, out_ref, s_scr):
    c = pl.program_id(1)
    bsz = qa_ref.shape[0]
    nb = vt_ref.shape[2]
    for n_groups in range(1, nb // ATTN_GROUP + 1):
        @pl.when(c == n_groups - 1)
        def _():
            def tile(t, carry):
                j = c * ATTN_GROUP + t
                _run_staggered([_attn_tile(qa_ref, ka_ref, vt_ref, bias_ref, out_ref, s_scr, b, t, j, n_groups)
                                for b in range(bsz)])
                return carry
            lax.fori_loop(0, ATTN_GROUP, tile, 0)


def _moba_attn(qa, ka, vt, bias):
    bsz, seq, _ = ka.shape
    nb = seq // MOBA_BLOCK
    assert nb % ATTN_GROUP == 0
    width = 2 * HEAD_DIM
    return pl.pallas_call(
        _moba_attn_kernel,
        grid=(HEADS, nb // ATTN_GROUP),
        in_specs=[
            pl.BlockSpec((bsz, 1, ATTN_GROUP, width, MOBA_BLOCK), lambda h, c: (0, h, c, 0, 0)),
            pl.BlockSpec((bsz, seq, width), lambda h, c: (0, 0, h)),
            pl.BlockSpec((bsz, 1, nb, HEAD_DIM, MOBA_BLOCK), lambda h, c: (0, h, 0, 0, 0)),
            pl.BlockSpec((N_BIAS_TILES, 1, MOBA_BLOCK, MOBA_BLOCK), lambda h, c: (0, h, 0, 0)),
        ],
        out_specs=pl.BlockSpec((bsz, ATTN_GROUP * MOBA_BLOCK, HEAD_DIM), lambda h, c: (0, c, h)),
        out_shape=jax.ShapeDtypeStruct((bsz, seq, D_MODEL), BF16),
        scratch_shapes=[pltpu.VMEM((bsz, 2, ATTN_GROUP, MOBA_BLOCK, MOBA_BLOCK), F32)],
        compiler_params=pltpu.CompilerParams(
            dimension_semantics=("arbitrary", "arbitrary"), vmem_limit_bytes=VMEM_LIMIT),
        name="moba_attn",
    )(qa, ka, vt, bias)


def kernel(x, norm_mix, norm_ffn, hgrn_w_in, hgrn_lb_logits, hgrn_out_norm, hgrn_w_out,
           moba_w_in, moba_w_out, rel_bias_table, ffn_w13, ffn_w2, final_norm):
    depth = norm_mix.shape[0]
    n_mixers = 2
    hgrn_w_in, hgrn_w_out, moba_w_in, moba_w_out, ffn_w13, ffn_w2 = (
        w.astype(BF16) for w in (hgrn_w_in, hgrn_w_out, moba_w_in, moba_w_out, ffn_w13, ffn_w2))
    norm_mix, norm_ffn, hgrn_out_norm = (g[:, None, :] for g in (norm_mix, norm_ffn, hgrn_out_norm))
    bias = _bias_tiles(rel_bias_table)
    h = x
    for layer in range(depth):
        idx = layer // n_mixers
        if layer % n_mixers == 0:
            a = _hgrn_mixer(h, norm_mix, layer, hgrn_w_in, hgrn_lb_logits, hgrn_out_norm, idx)
            w_out = hgrn_w_out
        else:
            qa, ka, vt = _moba_proj(h, norm_mix, layer, moba_w_in, idx, rel_bias_table)
            a = _moba_attn(qa, ka, vt, bias)
            w_out = moba_w_out
        h = _outproj_ffn(h, a, w_out, idx, norm_ffn, ffn_w13, ffn_w2, layer, final_norm,
                         final_norm=(layer == depth - 1))
    return h
```

```python
import functools
import math

import numpy as np
import jax
import jax.numpy as jnp
from jax import lax
from jax.experimental import pallas as pl
from jax.experimental.pallas import tpu as pltpu

D_MODEL = 1024
HEADS = 8
HEAD_DIM = D_MODEL // HEADS
MOBA_BLOCK = 256
MOBA_TOPK = 3
REL_BUCKETS = 32
REL_MAX_DISTANCE = 1024
RMS_EPS = 1e-6
HGRN_CHUNK = 128
HGRN_CHUNKS_PER_STEP = 4
FFN_ROWS = 512
FFN_STAGE_SLABS = 8
ATTN_GROUP = 4
NEG = -1e30
N_BIAS_TILES = 6
VMEM_LIMIT = 56 * 1024 * 1024
LOG2E = math.log2(math.e)
SUBLANES = 8

BF16 = jnp.bfloat16
F32 = jnp.float32


def _resident(shape):
    zeros = (0,) * len(shape)
    return pl.BlockSpec(shape, lambda *_: zeros, pipeline_mode=pl.Buffered(1))


def _layer(stacked, idx):
    zeros = (0,) * (stacked.ndim - 1)
    return pl.BlockSpec((None,) + stacked.shape[1:], lambda *_: (idx,) + zeros, pipeline_mode=pl.Buffered(1))


def _sigmoid(x):
    return 1.0 / (1.0 + jnp.exp(-x))


def _rms(x, gain):
    return x * lax.rsqrt(jnp.mean(x * x, axis=-1, keepdims=True) + RMS_EPS) * gain


def _dot(a, b):
    return jnp.dot(a, b, preferred_element_type=F32)


def _run_staggered(sequences):
    live = list(enumerate(sequences))
    step = 0
    while live:
        for i, seq in list(live):
            if step >= i and next(seq, StopIteration) is StopIteration:
                live.remove((i, seq))
        step += 1


def _hgrn_tables(c):
    ms = [c >> (i + 1) for i in range(int(math.log2(c)))]
    r = np.arange(c)
    level = np.full((c, c), -1, np.int32)
    gsel = np.zeros((len(ms), c, c), np.float32)
    sgn = np.zeros((len(ms), c, HEAD_DIM), np.float32)
    for i, m in enumerate(ms):
        upper = (r % (2 * m)) >= m
        same = (r[:, None] // (2 * m)) == (r[None, :] // (2 * m))
        level[same & upper[:, None] & ~upper[None, :]] = i
        mid = (r // (2 * m)) * 2 * m + m - 1
        gsel[i, r, mid] = 1.0
        sgn[i] = np.where(upper, 1.0, -1.0)[:, None]
    level[r, r] = len(ms)
    tri = (r[:, None] >= r[None, :]).astype(np.float32)
    fine = [i for i, m in enumerate(ms) if 2 * m < SUBLANES]
    return (jnp.asarray(tri, BF16), jnp.asarray(gsel[fine].reshape(len(fine) * c, c), BF16), jnp.asarray(sgn, F32),
            jnp.asarray(level, jnp.int32), ms)


def _hgrn_kernel(x_ref, gain_ref, w_ref, lbl_ref, onorm_ref, tri_ref, gsel_ref, sgn_ref, lvl_ref,
                 a_ref, st_ref, *, layer_idx, ms):
    bsz, _, d = x_ref.shape
    c = HGRN_CHUNK
    n_levels = len(ms)

    @pl.when(pl.program_id(0) == 0)
    def _():
        st_ref[...] = jnp.zeros_like(st_ref)

    logits = [lbl_ref[r:r + 1, :] for r in range(lbl_ref.shape[0])]
    top = functools.reduce(jnp.maximum, logits)
    ex = [jnp.exp(t - top) for t in logits]
    lb = sum(ex[:layer_idx + 1]) / sum(ex)

    def chunk(b, p_all, rows):
        p = p_all[b * c:(b + 1) * c]
        qz = p[:, :d]
        q = qz * _sigmoid(qz)
        f = lb + (1.0 - lb) * _sigmoid(p[:, d:2 * d])
        k = 1.0 - f
        g = jnp.log(f) * LOG2E
        inp_b = p[:, 2 * d:3 * d].astype(BF16)
        og = _sigmoid(p[:, 3 * d:])
        g0 = g.astype(BF16)
        r1 = g - g0.astype(F32)
        g1 = r1.astype(BF16)
        g2 = (r1 - g1.astype(F32)).astype(BF16)
        yield
        tri = tri_ref[...]
        cum = _dot(tri, g0) + _dot(tri, g1) + _dot(tri, g2)
        fine_refs = _dot(gsel_ref[...], cum.astype(BF16))
        yield
        qe, ke = [], []
        for i, m in enumerate(ms):
            if 2 * m >= SUBLANES:
                blocks = cum.reshape(c // (2 * m), 2 * m, d)
                ref = jnp.broadcast_to(blocks[:, m - 1:m, :], blocks.shape).reshape(c, d)
            else:
                fi = i - (n_levels - gsel_ref.shape[0] // c)
                ref = fine_refs[fi * c:(fi + 1) * c]
            e = jnp.exp2((cum - ref) * jnp.tile(sgn_ref[i], (1, HEADS)))
            qe.append((q * e).astype(BF16))
            ke.append((k * e).T.astype(BF16))
        qe.append(q.astype(BF16))
        ke.append(k.T.astype(BF16))
        last = cum[c - 1:c, :]
        qs = (q * jnp.exp2(cum)).astype(BF16)
        kd = (k * jnp.exp2(last - cum)).T.astype(BF16)
        decay = jnp.broadcast_to(jnp.exp2(last), (SUBLANES, d)).T[:, 0:1]
        yield
        lvl = lvl_ref[...]
        outs = []
        for h in range(HEADS):
            cols = slice(h * HEAD_DIM, (h + 1) * HEAD_DIM)
            scores = jnp.zeros((c, c), F32)
            for i in range(n_levels + 1):
                scores = jnp.where(lvl == i, _dot(qe[i][:, cols], ke[i][cols, :]), scores)
            st = st_ref[b, h]
            o = _dot(scores.astype(BF16), inp_b[:, cols]) + _dot(qs[:, cols], st.astype(BF16))
            st_ref[b, h] = st * decay[cols, :] + _dot(kd[cols, :], inp_b[:, cols])
            outs.append(o)
        yield
        ss = sum(jnp.sum(o * o, axis=-1, keepdims=True) for o in outs)
        inv = lax.rsqrt(ss / D_MODEL + RMS_EPS)
        for h in range(HEADS):
            cols = slice(h * HEAD_DIM, (h + 1) * HEAD_DIM)
            a_ref[b, rows, cols] = (outs[h] * inv * onorm_ref[:, cols] * og[:, cols]).astype(a_ref.dtype)

    def step(t, carry):
        rows = pl.ds(pl.multiple_of(t * c, c), c)
        xn = _rms(x_ref[:, rows, :].reshape(bsz * c, d), gain_ref[...]).astype(BF16)
        p_all = _dot(xn, w_ref[...])
        _run_staggered([chunk(b, p_all, rows) for b in range(bsz)])
        return carry

    lax.fori_loop(0, x_ref.shape[1] // c, step, 0)


def _hgrn_mixer(x, gains, layer, w_in, lb_logits, out_norms, layer_idx):
    bsz, seq, _ = x.shape
    c = HGRN_CHUNK
    rows_per_step = c * HGRN_CHUNKS_PER_STEP
    tri, gsel, sgn, lvl, ms = _hgrn_tables(c)
    kern = functools.partial(_hgrn_kernel, layer_idx=layer_idx, ms=tuple(ms))
    return pl.pallas_call(
        kern,
        grid=(seq // rows_per_step,),
        in_specs=[
            pl.BlockSpec((bsz, rows_per_step, D_MODEL), lambda i: (0, i, 0)),
            _layer(gains, layer),
            _layer(w_in, layer_idx),
            _resident(lb_logits.shape),
            _layer(out_norms, layer_idx),
            _resident(tri.shape), _resident(gsel.shape), _resident(sgn.shape), _resident(lvl.shape),
        ],
        out_specs=pl.BlockSpec((bsz, rows_per_step, D_MODEL), lambda i: (0, i, 0)),
        out_shape=jax.ShapeDtypeStruct((bsz, seq, D_MODEL), BF16),
        scratch_shapes=[pltpu.VMEM((bsz, HEADS, HEAD_DIM, HEAD_DIM), F32)],
        compiler_params=pltpu.CompilerParams(
            dimension_semantics=("arbitrary",), vmem_limit_bytes=VMEM_LIMIT),
        name="hgrn_mixer",
    )(x, gains, w_in, lb_logits, out_norms, tri, gsel, sgn, lvl)


def _ffn_kernel(res_ref, a_ref, wo_ref, gain_ref, w13_hbm, w2_hbm, fin_ref, out_ref,
                w13_ref, w2_ref, stage13, stage2, *, final_norm, layer):
    d_ff = w2_ref.shape[0]

    @pl.when(pl.program_id(0) == 0)
    def _():
        for r in range(0, w13_ref.shape[0], stage13.shape[0]):
            pltpu.sync_copy(w13_hbm.at[layer, pl.ds(r, stage13.shape[0]), :], stage13)
            w13_ref[r:r + stage13.shape[0], :] = stage13[...].astype(BF16)
        for r in range(0, d_ff, stage2.shape[0]):
            pltpu.sync_copy(w2_hbm.at[layer, pl.ds(r, stage2.shape[0]), :], stage2)
            w2_ref[r:r + stage2.shape[0], :] = stage2[...].astype(BF16)

    h = res_ref[...] + _dot(a_ref[...], wo_ref[...])
    xn = _rms(h, gain_ref[...]).astype(BF16)
    gu = _dot(xn, w13_ref[...])
    g = gu[:, :d_ff]
    u = gu[:, d_ff:]
    act = (g * _sigmoid(g) * u).astype(BF16)
    h = h + _dot(act, w2_ref[...])
    if final_norm:
        h = _rms(h, fin_ref[...])
    out_ref[...] = h


def _outproj_ffn(res, a, w_out, idx, gains, w13, w2, layer, final_gain, final_norm):
    bsz, seq, _ = res.shape
    rows = bsz * seq
    kern = functools.partial(_ffn_kernel, final_norm=final_norm, layer=layer)
    row_spec = pl.BlockSpec((FFN_ROWS, D_MODEL), lambda i: (i, 0))
    d_ff = w2.shape[1]
    assert D_MODEL % FFN_STAGE_SLABS == 0 and d_ff % FFN_STAGE_SLABS == 0
    out = pl.pallas_call(
        kern,
        grid=(rows // FFN_ROWS,),
        in_specs=[
            row_spec, row_spec,
            _layer(w_out, idx),
            _layer(gains, layer),
            pl.BlockSpec(memory_space=pl.ANY),
            pl.BlockSpec(memory_space=pl.ANY),
            _resident((1, D_MODEL)),
        ],
        out_specs=row_spec,
        out_shape=jax.ShapeDtypeStruct((rows, D_MODEL), F32),
        scratch_shapes=[
            pltpu.VMEM((D_MODEL, 2 * d_ff), BF16),
            pltpu.VMEM((d_ff, D_MODEL), BF16),
            pltpu.VMEM((D_MODEL // FFN_STAGE_SLABS, 2 * d_ff), F32),
            pltpu.VMEM((d_ff // FFN_STAGE_SLABS, D_MODEL), F32),
        ],
        compiler_params=pltpu.CompilerParams(
            dimension_semantics=("arbitrary",), vmem_limit_bytes=VMEM_LIMIT),
        name="outproj_ffn_final" if final_norm else "outproj_ffn",
    )(res.reshape(rows, D_MODEL), a.reshape(rows, D_MODEL), w_out, gains, w13, w2, final_gain.reshape(1, D_MODEL))
    return out.reshape(bsz, seq, D_MODEL)


def _t5_thresholds():
    max_exact = REL_BUCKETS // 2
    n_log = REL_BUCKETS - max_exact
    ratio = REL_MAX_DISTANCE / max_exact
    return [int(math.ceil(max_exact * ratio ** (k / n_log) - 1e-9)) for k in range(1, n_log)]


def _t5_bucket(dist):
    max_exact = REL_BUCKETS // 2
    return dist if dist < max_exact else max_exact + sum(dist >= t for t in _t5_thresholds())


def _bias_kernel(tab_ref, out_ref):
    dl = pl.program_id(0)
    tk = lax.broadcasted_iota(jnp.int32, (MOBA_BLOCK, MOBA_BLOCK), 0)
    tq = lax.broadcasted_iota(jnp.int32, (MOBA_BLOCK, MOBA_BLOCK), 1)
    max_exact = REL_BUCKETS // 2
    for tile in range(N_BIAS_TILES - 1):
        @pl.when(dl == tile)
        def _():
            dist = tile * MOBA_BLOCK + tq - tk
            d_lo = max(tile * MOBA_BLOCK - (MOBA_BLOCK - 1), 0)
            d_hi = tile * MOBA_BLOCK + MOBA_BLOCK - 1
            b_lo, b_hi = _t5_bucket(d_lo), _t5_bucket(d_hi)
            large = jnp.full(dist.shape, max_exact, jnp.int32)
            for t in _t5_thresholds():
                if t <= d_hi:
                    large = large + (dist >= t).astype(jnp.int32)
            bucket = jnp.where(dist < max_exact, dist, large) if d_lo < max_exact else large
            for h in range(HEADS):
                bias = jnp.full(dist.shape, tab_ref[b_lo, h] * LOG2E, F32)
                for bkt in range(b_lo + 1, b_hi + 1):
                    bias = jnp.where(bucket == bkt, tab_ref[bkt, h] * LOG2E, bias)
                if tile == 0:
                    bias = jnp.where(dist < 0, NEG, bias)
                out_ref[0, h] = bias

    @pl.when(dl == N_BIAS_TILES - 1)
    def _():
        out_ref[...] = jnp.zeros_like(out_ref)


def _bias_tiles(rel_table):
    return pl.pallas_call(
        _bias_kernel,
        grid=(N_BIAS_TILES,),
        in_specs=[pl.BlockSpec(memory_space=pltpu.SMEM)],
        out_specs=pl.BlockSpec((1, HEADS, MOBA_BLOCK, MOBA_BLOCK), lambda i: (i, 0, 0, 0)),
        out_shape=jax.ShapeDtypeStruct((N_BIAS_TILES, HEADS, MOBA_BLOCK, MOBA_BLOCK), F32),
        compiler_params=pltpu.CompilerParams(dimension_semantics=("arbitrary",)),
        name="moba_bias_tiles",
    )(rel_table)


def _moba_proj_kernel(x_ref, gain_ref, w_ref, tab_ref, qa_ref, ka_ref, vt_ref, kmean_scr):
    j = pl.program_id(0)
    bsz, nb = kmean_scr.shape[0], kmean_scr.shape[2]
    width = 2 * HEAD_DIM
    col_head = lax.broadcasted_iota(jnp.int32, (1, D_MODEL), 1) // HEAD_DIM

    @pl.when(j == 0)
    def _():
        kmean_scr[...] = jnp.zeros_like(kmean_scr)

    xn = _rms(x_ref[...].reshape(bsz * MOBA_BLOCK, D_MODEL), gain_ref[...]).astype(BF16)
    qkv = _dot(xn, w_ref[...])

    lane = lax.broadcasted_iota(jnp.int32, (MOBA_BLOCK, HEAD_DIM), 1)
    onehot = ((lane == j) | (lane == j + nb)).astype(BF16)
    blk = lax.broadcasted_iota(jnp.int32, (HEADS, nb, MOBA_BLOCK), 1)
    blk_f = blk.astype(F32)
    far_bias = jnp.concatenate(
        [jnp.broadcast_to(tab_ref[REL_BUCKETS - 1:REL_BUCKETS, h:h + 1] * LOG2E, (1, nb, MOBA_BLOCK)) for h in range(HEADS)],
        axis=0)
    far_bias = jnp.where(j - blk >= N_BIAS_TILES - 1, far_bias, 0.0)
    pad = jnp.zeros((HEAD_DIM - 2 * nb, MOBA_BLOCK), F32)
    for b in range(bsz):
        rows = slice(b * MOBA_BLOCK, (b + 1) * MOBA_BLOCK)
        q = qkv[rows, :D_MODEL]
        k = qkv[rows, D_MODEL:2 * D_MODEL]
        v = qkv[rows, 2 * D_MODEL:]
        vt_ref[b, :, 0] = v.T.reshape(HEADS, HEAD_DIM, MOBA_BLOCK).astype(BF16)
        qt_all = q.T
        km = kmean_scr[b].reshape(HEADS * nb, D_MODEL)
        km_hi = km.astype(BF16)
        km_lo = (km - km_hi.astype(F32)).astype(BF16)
        qt_hi = qt_all.astype(BF16)
        qt_lo = (qt_all - qt_hi.astype(F32)).astype(BF16)
        gate = (_dot(km_hi, qt_hi) + _dot(km_hi, qt_lo) + _dot(km_lo, qt_hi)).reshape(HEADS, nb, MOBA_BLOCK)
        gate = jnp.where(blk < j, gate, -jnp.inf)
        madd = jnp.full(gate.shape, NEG, F32)
        for _ in range(MOBA_TOPK):
            top = jnp.max(gate, axis=1, keepdims=True)
            idx = jnp.min(jnp.where(gate == top, blk_f, float(nb)), axis=1, keepdims=True)
            hit = blk_f == idx
            madd = jnp.where(hit & (top > -jnp.inf), far_bias, madd)
            gate = jnp.where(hit, -jnp.inf, gate)
        madd = jnp.where(blk == j, 0.0, madd)
        hi = madd.astype(BF16).astype(F32)
        lo = (madd - hi).astype(BF16).astype(F32)
        for h in range(HEADS):
            cols = slice(h * HEAD_DIM, (h + 1) * HEAD_DIM)
            qt = qt_all[cols, :] * (LOG2E * HEAD_DIM ** -0.5)
            qa_ref[b, h, 0] = jnp.concatenate([qt, hi[h], lo[h], pad], axis=0).astype(BF16)
            ka_ref[b, :, h * width:h * width + HEAD_DIM] = k[:, cols].astype(BF16)
            ka_ref[b, :, h * width + HEAD_DIM:(h + 1) * width] = onehot
        kmean = jnp.mean(k, axis=0, keepdims=True)
        for h in range(HEADS):
            kmean_scr[b, h, pl.ds(j, 1), :] = jnp.where(col_head == h, kmean, 0.0)


def _moba_proj(h, gains, layer, w_in, idx, rel_table):
    bsz, seq, _ = h.shape
    nb = seq // MOBA_BLOCK
    assert 2 * nb <= HEAD_DIM
    return pl.pallas_call(
        _moba_proj_kernel,
        grid=(nb,),
        in_specs=[
            pl.BlockSpec((bsz, MOBA_BLOCK, D_MODEL), lambda j: (0, j, 0)),
            _layer(gains, layer),
            _layer(w_in, idx),
            _resident(rel_table.shape),
        ],
        out_specs=[
            pl.BlockSpec((bsz, HEADS, 1, 2 * HEAD_DIM, MOBA_BLOCK), lambda j: (0, 0, j, 0, 0)),
            pl.BlockSpec((bsz, MOBA_BLOCK, 2 * D_MODEL), lambda j: (0, j, 0)),
            pl.BlockSpec((bsz, HEADS, 1, HEAD_DIM, MOBA_BLOCK), lambda j: (0, 0, j, 0, 0)),
        ],
        out_shape=[
            jax.ShapeDtypeStruct((bsz, HEADS, nb, 2 * HEAD_DIM, MOBA_BLOCK), BF16),
            jax.ShapeDtypeStruct((bsz, seq, 2 * D_MODEL), BF16),
            jax.ShapeDtypeStruct((bsz, HEADS, nb, HEAD_DIM, MOBA_BLOCK), BF16),
        ],
        scratch_shapes=[pltpu.VMEM((bsz, HEADS, nb, D_MODEL), F32)],
        compiler_params=pltpu.CompilerParams(
            dimension_semantics=("arbitrary",), vmem_limit_bytes=VMEM_LIMIT),
        name="moba_proj",
    )(h, gains, w_in, rel_table)


def _attn_tile(qa_ref, ka_ref, vt_ref, bias_ref, out_ref, s_scr, b, t, j, n_groups):
    qa = qa_ref[b, 0, t]
    fold = (MOBA_BLOCK // SUBLANES, SUBLANES, MOBA_BLOCK)
    first_near = (n_groups - 1) * ATTN_GROUP - (N_BIAS_TILES - 2)

    def qk_group(g):
        gmax = None
        for u in range(ATTN_GROUP):
            n = g * ATTN_GROUP + u
            s = _dot(ka_ref[b, n * MOBA_BLOCK:(n + 1) * MOBA_BLOCK, :], qa)
            if n >= first_near:
                s = s + bias_ref[jnp.clip(j - n, 0, N_BIAS_TILES - 1), 0]
            s_scr[b, g % 2, u] = s
            tmax = jnp.max(s.reshape(fold), axis=0)
            gmax = tmax if gmax is None else jnp.maximum(gmax, tmax)
        return gmax

    m = jnp.full((1, MOBA_BLOCK), NEG, F32)
    lrun = jnp.zeros((SUBLANES, MOBA_BLOCK), F32)
    acc = jnp.zeros((HEAD_DIM, MOBA_BLOCK), F32)
    gmax = qk_group(0)
    yield
    for g in range(n_groups):
        nxt = qk_group(g + 1) if g + 1 < n_groups else None
        yield
        m_new = jnp.maximum(m, jnp.max(gmax, axis=0, keepdims=True))
        alpha = jnp.exp2(m - m_new)
        lsum = jnp.zeros((SUBLANES, MOBA_BLOCK), F32)
        pv = jnp.zeros((HEAD_DIM, MOBA_BLOCK), F32)
        for u in range(ATTN_GROUP):
            p = jnp.exp2(s_scr[b, g % 2, u] - m_new)
            lsum = lsum + jnp.sum(p.reshape(fold), axis=0)
            pv = pv + _dot(vt_ref[b, 0, g * ATTN_GROUP + u], p.astype(BF16))
        lrun = alpha * lrun + lsum
        acc = alpha * acc + pv
        m = m_new
        gmax = nxt
        yield
    l = jnp.sum(lrun, axis=0, keepdims=True)
    out_ref[b, pl.ds(pl.multiple_of(t * MOBA_BLOCK, MOBA_BLOCK), MOBA_BLOCK), :] = (acc / l).T.astype(out_ref.dtype)


def _moba_attn_kernel(qa_ref, ka_ref, vt_ref, bias_ref, out_ref, s_scr):
    c = pl.program_id(1)
    bsz = qa_ref.shape[0]
    nb = vt_ref.shape[2]
    for n_groups in range(1, nb // ATTN_GROUP + 1):
        @pl.when(c == n_groups - 1)
        def _():
            def tile(t, carry):
                j = c * ATTN_GROUP + t
                _run_staggered([_attn_tile(qa_ref, ka_ref, vt_ref, bias_ref, out_ref, s_scr, b, t, j, n_groups)
                                for b in range(bsz)])
                return carry
            lax.fori_loop(0, ATTN_GROUP, tile, 0)


def _moba_attn(qa, ka, vt, bias):
    bsz, seq, _ = ka.shape
    nb = seq // MOBA_BLOCK
    assert nb % ATTN_GROUP == 0
    width = 2 * HEAD_DIM
    return pl.pallas_call(
        _moba_attn_kernel,
        grid=(HEADS, nb // ATTN_GROUP),
        in_specs=[
            pl.BlockSpec((bsz, 1, ATTN_GROUP, width, MOBA_BLOCK), lambda h, c: (0, h, c, 0, 0)),
            pl.BlockSpec((bsz, seq, width), lambda h, c: (0, 0, h)),
            pl.BlockSpec((bsz, 1, nb, HEAD_DIM, MOBA_BLOCK), lambda h, c: (0, h, 0, 0, 0)),
            pl.BlockSpec((N_BIAS_TILES, 1, MOBA_BLOCK, MOBA_BLOCK), lambda h, c: (0, h, 0, 0)),
        ],
        out_specs=pl.BlockSpec((bsz, ATTN_GROUP * MOBA_BLOCK, HEAD_DIM), lambda h, c: (0, c, h)),
        out_shape=jax.ShapeDtypeStruct((bsz, seq, D_MODEL), BF16),
        scratch_shapes=[pltpu.VMEM((bsz, 2, ATTN_GROUP, MOBA_BLOCK, MOBA_BLOCK), F32)],
        compiler_params=pltpu.CompilerParams(
            dimension_semantics=("arbitrary", "arbitrary"), vmem_limit_bytes=VMEM_LIMIT),
        name="moba_attn",
    )(qa, ka, vt, bias)


def kernel(x, norm_mix, norm_ffn, hgrn_w_in, hgrn_lb_logits, hgrn_out_norm, hgrn_w_out,
           moba_w_in, moba_w_out, rel_bias_table, ffn_w13, ffn_w2, final_norm):
    depth = norm_mix.shape[0]
    n_mixers = 2
    hgrn_w_in, hgrn_w_out, moba_w_in, moba_w_out = (
        w.astype(BF16) for w in (hgrn_w_in, hgrn_w_out, moba_w_in, moba_w_out))
    norm_mix, norm_ffn, hgrn_out_norm = (g[:, None, :] for g in (norm_mix, norm_ffn, hgrn_out_norm))
    bias = _bias_tiles(rel_bias_table)
    h = x
    for layer in range(depth):
        idx = layer // n_mixers
        if layer % n_mixers == 0:
            a = _hgrn_mixer(h, norm_mix, layer, hgrn_w_in, hgrn_lb_logits, hgrn_out_norm, idx)
            w_out = hgrn_w_out
        else:
            qa, ka, vt = _moba_proj(h, norm_mix, layer, moba_w_in, idx, rel_bias_table)
            a = _moba_attn(qa, ka, vt, bias)
            w_out = moba_w_out
        h = _outproj_ffn(h, a, w_out, idx, norm_ffn, ffn_w13, ffn_w2, layer, final_norm,
                         final_norm=(layer == depth - 1))
    return h
```

```python
import functools
import math

import numpy as np
import jax
import jax.numpy as jnp
from jax import lax
from jax.experimental import pallas as pl
from jax.experimental.pallas import tpu as pltpu

D_MODEL = 1024
HEADS = 8
HEAD_DIM = D_MODEL // HEADS
MOBA_BLOCK = 256
MOBA_TOPK = 3
REL_BUCKETS = 32
REL_MAX_DISTANCE = 1024
RMS_EPS = 1e-6
HGRN_CHUNK = 128
HGRN_CHUNKS_PER_STEP = 4
FFN_ROWS = 512
FFN_STAGE_SLABS = 16
ATTN_GROUP = 4
NEG = -1e30
N_BIAS_TILES = 6
VMEM_LIMIT = 56 * 1024 * 1024
LOG2E = math.log2(math.e)
SUBLANES = 8

BF16 = jnp.bfloat16
F32 = jnp.float32


def _resident(shape):
    zeros = (0,) * len(shape)
    return pl.BlockSpec(shape, lambda *_: zeros, pipeline_mode=pl.Buffered(1))


def _layer(stacked, idx):
    zeros = (0,) * (stacked.ndim - 1)
    return pl.BlockSpec((None,) + stacked.shape[1:], lambda *_: (idx,) + zeros, pipeline_mode=pl.Buffered(1))


def _sigmoid(x):
    return 1.0 / (1.0 + jnp.exp(-x))


def _rms(x, gain):
    return x * lax.rsqrt(jnp.mean(x * x, axis=-1, keepdims=True) + RMS_EPS) * gain


def _dot(a, b):
    return jnp.dot(a, b, preferred_element_type=F32)


def _run_staggered(sequences):
    live = list(enumerate(sequences))
    step = 0
    while live:
        for i, seq in list(live):
            if step >= i and next(seq, StopIteration) is StopIteration:
                live.remove((i, seq))
        step += 1


def _hgrn_tables(c):
    ms = [c >> (i + 1) for i in range(int(math.log2(c)))]
    r = np.arange(c)
    level = np.full((c, c), -1, np.int32)
    gsel = np.zeros((len(ms), c, c), np.float32)
    sgn = np.zeros((len(ms), c, HEAD_DIM), np.float32)
    for i, m in enumerate(ms):
        upper = (r % (2 * m)) >= m
        same = (r[:, None] // (2 * m)) == (r[None, :] // (2 * m))
        level[same & upper[:, None] & ~upper[None, :]] = i
        mid = (r // (2 * m)) * 2 * m + m - 1
        gsel[i, r, mid] = 1.0
        sgn[i] = np.where(upper, 1.0, -1.0)[:, None]
    level[r, r] = len(ms)
    tri = (r[:, None] >= r[None, :]).astype(np.float32)
    fine = [i for i, m in enumerate(ms) if 2 * m < SUBLANES]
    return (jnp.asarray(tri, BF16), jnp.asarray(gsel[fine].reshape(len(fine) * c, c), BF16), jnp.asarray(sgn, F32),
            jnp.asarray(level, jnp.int32), ms)


def _hgrn_kernel(x_ref, gain_ref, w_ref, lbl_ref, onorm_ref, tri_ref, gsel_ref, sgn_ref, lvl_ref,
                 a_ref, st_ref, *, layer_idx, ms):
    bsz, _, d = x_ref.shape
    c = HGRN_CHUNK
    n_levels = len(ms)

    @pl.when(pl.program_id(0) == 0)
    def _():
        st_ref[...] = jnp.zeros_like(st_ref)

    logits = [lbl_ref[r:r + 1, :] for r in range(lbl_ref.shape[0])]
    top = functools.reduce(jnp.maximum, logits)
    ex = [jnp.exp(t - top) for t in logits]
    lb = sum(ex[:layer_idx + 1]) / sum(ex)

    def chunk(b, p_all, rows):
        p = p_all[b * c:(b + 1) * c]
        qz = p[:, :d]
        q = qz * _sigmoid(qz)
        f = lb + (1.0 - lb) * _sigmoid(p[:, d:2 * d])
        k = 1.0 - f
        g = jnp.log(f) * LOG2E
        inp_b = p[:, 2 * d:3 * d].astype(BF16)
        og = _sigmoid(p[:, 3 * d:])
        g0 = g.astype(BF16)
        r1 = g - g0.astype(F32)
        g1 = r1.astype(BF16)
        g2 = (r1 - g1.astype(F32)).astype(BF16)
        yield
        tri = tri_ref[...]
        cum = _dot(tri, g0) + _dot(tri, g1) + _dot(tri, g2)
        fine_refs = _dot(gsel_ref[...], cum.astype(BF16))
        yield
        qe, ke = [], []
        for i, m in enumerate(ms):
            if 2 * m >= SUBLANES:
                blocks = cum.reshape(c // (2 * m), 2 * m, d)
                ref = jnp.broadcast_to(blocks[:, m - 1:m, :], blocks.shape).reshape(c, d)
            else:
                fi = i - (n_levels - gsel_ref.shape[0] // c)
                ref = fine_refs[fi * c:(fi + 1) * c]
            e = jnp.exp2((cum - ref) * jnp.tile(sgn_ref[i], (1, HEADS)))
            qe.append((q * e).astype(BF16))
            ke.append((k * e).T.astype(BF16))
        qe.append(q.astype(BF16))
        ke.append(k.T.astype(BF16))
        last = cum[c - 1:c, :]
        qs = (q * jnp.exp2(cum)).astype(BF16)
        kd = (k * jnp.exp2(last - cum)).T.astype(BF16)
        decay = jnp.broadcast_to(jnp.exp2(last), (SUBLANES, d)).T[:, 0:1]
        yield
        lvl = lvl_ref[...]
        outs = []
        for h in range(HEADS):
            cols = slice(h * HEAD_DIM, (h + 1) * HEAD_DIM)
            scores = jnp.zeros((c, c), F32)
            for i in range(n_levels + 1):
                scores = jnp.where(lvl == i, _dot(qe[i][:, cols], ke[i][cols, :]), scores)
            st = st_ref[b, h]
            o = _dot(scores.astype(BF16), inp_b[:, cols]) + _dot(qs[:, cols], st.astype(BF16))
            st_ref[b, h] = st * decay[cols, :] + _dot(kd[cols, :], inp_b[:, cols])
            outs.append(o)
        yield
        ss = sum(jnp.sum(o * o, axis=-1, keepdims=True) for o in outs)
        inv = lax.rsqrt(ss / D_MODEL + RMS_EPS)
        for h in range(HEADS):
            cols = slice(h * HEAD_DIM, (h + 1) * HEAD_DIM)
            a_ref[b, rows, cols] = (outs[h] * inv * onorm_ref[:, cols] * og[:, cols]).astype(a_ref.dtype)

    def step(t, carry):
        rows = pl.ds(pl.multiple_of(t * c, c), c)
        xn = _rms(x_ref[:, rows, :].reshape(bsz * c, d), gain_ref[...]).astype(BF16)
        p_all = _dot(xn, w_ref[...])
        _run_staggered([chunk(b, p_all, rows) for b in range(bsz)])
        return carry

    lax.fori_loop(0, x_ref.shape[1] // c, step, 0)


def _hgrn_mixer(x, gains, layer, w_in, lb_logits, out_norms, layer_idx):
    bsz, seq, _ = x.shape
    c = HGRN_CHUNK
    rows_per_step = c * HGRN_CHUNKS_PER_STEP
    tri, gsel, sgn, lvl, ms = _hgrn_tables(c)
    kern = functools.partial(_hgrn_kernel, layer_idx=layer_idx, ms=tuple(ms))
    return pl.pallas_call(
        kern,
        grid=(seq // rows_per_step,),
        in_specs=[
            pl.BlockSpec((bsz, rows_per_step, D_MODEL), lambda i: (0, i, 0)),
            _layer(gains, layer),
            _layer(w_in, layer_idx),
            _resident(lb_logits.shape),
            _layer(out_norms, layer_idx),
            _resident(tri.shape), _resident(gsel.shape), _resident(sgn.shape), _resident(lvl.shape),
        ],
        out_specs=pl.BlockSpec((bsz, rows_per_step, D_MODEL), lambda i: (0, i, 0)),
        out_shape=jax.ShapeDtypeStruct((bsz, seq, D_MODEL), BF16),
        scratch_shapes=[pltpu.VMEM((bsz, HEADS, HEAD_DIM, HEAD_DIM), F32)],
        compiler_params=pltpu.CompilerParams(
            dimension_semantics=("arbitrary",), vmem_limit_bytes=VMEM_LIMIT),
        name="hgrn_mixer",
    )(x, gains, w_in, lb_logits, out_norms, tri, gsel, sgn, lvl)


def _stage_weights(w_hbm, layer, stage, sems, dst_ref):
    rows = stage.shape[1]
    n_slabs = dst_ref.shape[0] // rows

    def fetch(i):
        return pltpu.make_async_copy(w_hbm.at[layer, pl.ds(i * rows, rows), :], stage.at[i % 2], sems.at[i % 2])

    fetch(0).start()
    for i in range(n_slabs):
        if i + 1 < n_slabs:
            fetch(i + 1).start()
        fetch(i).wait()
        dst_ref[i * rows:(i + 1) * rows, :] = stage[i % 2].astype(BF16)


def _ffn_kernel(res_ref, a_ref, wo_ref, gain_ref, w13_hbm, w2_hbm, fin_ref, out_ref,
                w13_ref, w2_ref, stage13, stage2, sems13, sems2, *, final_norm, layer):
    d_ff = w2_ref.shape[0]

    @pl.when(pl.program_id(0) == 0)
    def _():
        _stage_weights(w13_hbm, layer, stage13, sems13, w13_ref)
        _stage_weights(w2_hbm, layer, stage2, sems2, w2_ref)

    h = res_ref[...] + _dot(a_ref[...], wo_ref[...])
    xn = _rms(h, gain_ref[...]).astype(BF16)
    gu = _dot(xn, w13_ref[...])
    g = gu[:, :d_ff]
    u = gu[:, d_ff:]
    act = (g * _sigmoid(g) * u).astype(BF16)
    h = h + _dot(act, w2_ref[...])
    if final_norm:
        h = _rms(h, fin_ref[...])
    out_ref[...] = h


def _outproj_ffn(res, a, w_out, idx, gains, w13, w2, layer, final_gain, final_norm):
    bsz, seq, _ = res.shape
    rows = bsz * seq
    kern = functools.partial(_ffn_kernel, final_norm=final_norm, layer=layer)
    row_spec = pl.BlockSpec((FFN_ROWS, D_MODEL), lambda i: (i, 0))
    d_ff = w2.shape[1]
    assert D_MODEL % FFN_STAGE_SLABS == 0 and d_ff % FFN_STAGE_SLABS == 0
    out = pl.pallas_call(
        kern,
        grid=(rows // FFN_ROWS,),
        in_specs=[
            row_spec, row_spec,
            _layer(w_out, idx),
            _layer(gains, layer),
            pl.BlockSpec(memory_space=pl.ANY),
            pl.BlockSpec(memory_space=pl.ANY),
            _resident((1, D_MODEL)),
        ],
        out_specs=row_spec,
        out_shape=jax.ShapeDtypeStruct((rows, D_MODEL), F32),
        scratch_shapes=[
            pltpu.VMEM((D_MODEL, 2 * d_ff), BF16),
            pltpu.VMEM((d_ff, D_MODEL), BF16),
            pltpu.VMEM((2, D_MODEL // FFN_STAGE_SLABS, 2 * d_ff), F32),
            pltpu.VMEM((2, d_ff // FFN_STAGE_SLABS, D_MODEL), F32),
            pltpu.SemaphoreType.DMA((2,)),
            pltpu.SemaphoreType.DMA((2,)),
        ],
        compiler_params=pltpu.CompilerParams(
            dimension_semantics=("arbitrary",), vmem_limit_bytes=VMEM_LIMIT),
        name="outproj_ffn_final" if final_norm else "outproj_ffn",
    )(res.reshape(rows, D_MODEL), a.reshape(rows, D_MODEL), w_out, gains, w13, w2, final_gain.reshape(1, D_MODEL))
    return out.reshape(bsz, seq, D_MODEL)


def _t5_thresholds():
    max_exact = REL_BUCKETS // 2
    n_log = REL_BUCKETS - max_exact
    ratio = REL_MAX_DISTANCE / max_exact
    return [int(math.ceil(max_exact * ratio ** (k / n_log) - 1e-9)) for k in range(1, n_log)]


def _t5_bucket(dist):
    max_exact = REL_BUCKETS // 2
    return dist if dist < max_exact else max_exact + sum(dist >= t for t in _t5_thresholds())


def _bias_kernel(tab_ref, out_ref):
    dl = pl.program_id(0)
    tk = lax.broadcasted_iota(jnp.int32, (MOBA_BLOCK, MOBA_BLOCK), 0)
    tq = lax.broadcasted_iota(jnp.int32, (MOBA_BLOCK, MOBA_BLOCK), 1)
    max_exact = REL_BUCKETS // 2
    for tile in range(N_BIAS_TILES - 1):
        @pl.when(dl == tile)
        def _():
            dist = tile * MOBA_BLOCK + tq - tk
            d_lo = max(tile * MOBA_BLOCK - (MOBA_BLOCK - 1), 0)
            d_hi = tile * MOBA_BLOCK + MOBA_BLOCK - 1
            b_lo, b_hi = _t5_bucket(d_lo), _t5_bucket(d_hi)
            large = jnp.full(dist.shape, max_exact, jnp.int32)
            for t in _t5_thresholds():
                if t <= d_hi:
                    large = large + (dist >= t).astype(jnp.int32)
            bucket = jnp.where(dist < max_exact, dist, large) if d_lo < max_exact else large
            for h in range(HEADS):
                bias = jnp.full(dist.shape, tab_ref[b_lo, h] * LOG2E, F32)
                for bkt in range(b_lo + 1, b_hi + 1):
                    bias = jnp.where(bucket == bkt, tab_ref[bkt, h] * LOG2E, bias)
                if tile == 0:
                    bias = jnp.where(dist < 0, NEG, bias)
                out_ref[0, h] = bias

    @pl.when(dl == N_BIAS_TILES - 1)
    def _():
        out_ref[...] = jnp.zeros_like(out_ref)


def _bias_tiles(rel_table):
    return pl.pallas_call(
        _bias_kernel,
        grid=(N_BIAS_TILES,),
        in_specs=[pl.BlockSpec(memory_space=pltpu.SMEM)],
        out_specs=pl.BlockSpec((1, HEADS, MOBA_BLOCK, MOBA_BLOCK), lambda i: (i, 0, 0, 0)),
        out_shape=jax.ShapeDtypeStruct((N_BIAS_TILES, HEADS, MOBA_BLOCK, MOBA_BLOCK), F32),
        compiler_params=pltpu.CompilerParams(dimension_semantics=("arbitrary",)),
        name="moba_bias_tiles",
    )(rel_table)


def _moba_proj_kernel(x_ref, gain_ref, w_ref, tab_ref, qa_ref, ka_ref, vt_ref, kmean_scr):
    j = pl.program_id(0)
    bsz, nb = kmean_scr.shape[0], kmean_scr.shape[2]
    width = 2 * HEAD_DIM
    col_head = lax.broadcasted_iota(jnp.int32, (1, D_MODEL), 1) // HEAD_DIM

    @pl.when(j == 0)
    def _():
        kmean_scr[...] = jnp.zeros_like(kmean_scr)

    xn = _rms(x_ref[...].reshape(bsz * MOBA_BLOCK, D_MODEL), gain_ref[...]).astype(BF16)
    qkv = _dot(xn, w_ref[...])

    lane = lax.broadcasted_iota(jnp.int32, (MOBA_BLOCK, HEAD_DIM), 1)
    onehot = ((lane == j) | (lane == j + nb)).astype(BF16)
    blk = lax.broadcasted_iota(jnp.int32, (HEADS, nb, MOBA_BLOCK), 1)
    blk_f = blk.astype(F32)
    far_bias = jnp.concatenate(
        [jnp.broadcast_to(tab_ref[REL_BUCKETS - 1:REL_BUCKETS, h:h + 1] * LOG2E, (1, nb, MOBA_BLOCK)) for h in range(HEADS)],
        axis=0)
    far_bias = jnp.where(j - blk >= N_BIAS_TILES - 1, far_bias, 0.0)
    pad = jnp.zeros((HEAD_DIM - 2 * nb, MOBA_BLOCK), F32)
    for b in range(bsz):
        rows = slice(b * MOBA_BLOCK, (b + 1) * MOBA_BLOCK)
        q = qkv[rows, :D_MODEL]
        k = qkv[rows, D_MODEL:2 * D_MODEL]
        v = qkv[rows, 2 * D_MODEL:]
        vt_ref[b, :, 0] = v.T.reshape(HEADS, HEAD_DIM, MOBA_BLOCK).astype(BF16)
        qt_all = q.T
        km = kmean_scr[b].reshape(HEADS * nb, D_MODEL)
        km_hi = km.astype(BF16)
        km_lo = (km - km_hi.astype(F32)).astype(BF16)
        qt_hi = qt_all.astype(BF16)
        qt_lo = (qt_all - qt_hi.astype(F32)).astype(BF16)
        gate = (_dot(km_hi, qt_hi) + _dot(km_hi, qt_lo) + _dot(km_lo, qt_hi)).reshape(HEADS, nb, MOBA_BLOCK)
        gate = jnp.where(blk < j, gate, -jnp.inf)
        madd = jnp.full(gate.shape, NEG, F32)
        for _ in range(MOBA_TOPK):
            top = jnp.max(gate, axis=1, keepdims=True)
            idx = jnp.min(jnp.where(gate == top, blk_f, float(nb)), axis=1, keepdims=True)
            hit = blk_f == idx
            madd = jnp.where(hit & (top > -jnp.inf), far_bias, madd)
            gate = jnp.where(hit, -jnp.inf, gate)
        madd = jnp.where(blk == j, 0.0, madd)
        hi = madd.astype(BF16).astype(F32)
        lo = (madd - hi).astype(BF16).astype(F32)
        for h in range(HEADS):
            cols = slice(h * HEAD_DIM, (h + 1) * HEAD_DIM)
            qt = qt_all[cols, :] * (LOG2E * HEAD_DIM ** -0.5)
            qa_ref[b, h, 0] = jnp.concatenate([qt, hi[h], lo[h], pad], axis=0).astype(BF16)
            ka_ref[b, :, h * width:h * width + HEAD_DIM] = k[:, cols].astype(BF16)
            ka_ref[b, :, h * width + HEAD_DIM:(h + 1) * width] = onehot
        kmean = jnp.mean(k, axis=0, keepdims=True)
        for h in range(HEADS):
            kmean_scr[b, h, pl.ds(j, 1), :] = jnp.where(col_head == h, kmean, 0.0)


def _moba_proj(h, gains, layer, w_in, idx, rel_table):
    bsz, seq, _ = h.shape
    nb = seq // MOBA_BLOCK
    assert 2 * nb <= HEAD_DIM
    return pl.pallas_call(
        _moba_proj_kernel,
        grid=(nb,),
        in_specs=[
            pl.BlockSpec((bsz, MOBA_BLOCK, D_MODEL), lambda j: (0, j, 0)),
            _layer(gains, layer),
            _layer(w_in, idx),
            _resident(rel_table.shape),
        ],
        out_specs=[
            pl.BlockSpec((bsz, HEADS, 1, 2 * HEAD_DIM, MOBA_BLOCK), lambda j: (0, 0, j, 0, 0)),
            pl.BlockSpec((bsz, MOBA_BLOCK, 2 * D_MODEL), lambda j: (0, j, 0)),
            pl.BlockSpec((bsz, HEADS, 1, HEAD_DIM, MOBA_BLOCK), lambda j: (0, 0, j, 0, 0)),
        ],
        out_shape=[
            jax.ShapeDtypeStruct((bsz, HEADS, nb, 2 * HEAD_DIM, MOBA_BLOCK), BF16),
            jax.ShapeDtypeStruct((bsz, seq, 2 * D_MODEL), BF16),
            jax.ShapeDtypeStruct((bsz, HEADS, nb, HEAD_DIM, MOBA_BLOCK), BF16),
        ],
        scratch_shapes=[pltpu.VMEM((bsz, HEADS, nb, D_MODEL), F32)],
        compiler_params=pltpu.CompilerParams(
            dimension_semantics=("arbitrary",), vmem_limit_bytes=VMEM_LIMIT),
        name="moba_proj",
    )(h, gains, w_in, rel_table)


def _attn_tile(qa_ref, ka_ref, vt_ref, bias_ref, out_ref, s_scr, b, t, j, n_groups):
    qa = qa_ref[b, 0, t]
    fold = (MOBA_BLOCK // SUBLANES, SUBLANES, MOBA_BLOCK)
    first_near = (n_groups - 1) * ATTN_GROUP - (N_BIAS_TILES - 2)

    def qk_group(g):
        gmax = None
        for u in range(ATTN_GROUP):
            n = g * ATTN_GROUP + u
            s = _dot(ka_ref[b, n * MOBA_BLOCK:(n + 1) * MOBA_BLOCK, :], qa)
            if n >= first_near:
                s = s + bias_ref[jnp.clip(j - n, 0, N_BIAS_TILES - 1), 0]
            s_scr[b, g % 2, u] = s
            tmax = jnp.max(s.reshape(fold), axis=0)
            gmax = tmax if gmax is None else jnp.maximum(gmax, tmax)
        return gmax

    m = jnp.full((1, MOBA_BLOCK), NEG, F32)
    lrun = jnp.zeros((SUBLANES, MOBA_BLOCK), F32)
    acc = jnp.zeros((HEAD_DIM, MOBA_BLOCK), F32)
    gmax = qk_group(0)
    yield
    for g in range(n_groups):
        nxt = qk_group(g + 1) if g + 1 < n_groups else None
        yield
        m_new = jnp.maximum(m, jnp.max(gmax, axis=0, keepdims=True))
        alpha = jnp.exp2(m - m_new)
        lsum = jnp.zeros((SUBLANES, MOBA_BLOCK), F32)
        pv = jnp.zeros((HEAD_DIM, MOBA_BLOCK), F32)
        for u in range(ATTN_GROUP):
            p = jnp.exp2(s_scr[b, g % 2, u] - m_new)
            lsum = lsum + jnp.sum(p.reshape(fold), axis=0)
            pv = pv + _dot(vt_ref[b, 0, g * ATTN_GROUP + u], p.astype(BF16))
        lrun = alpha * lrun + lsum
        acc = alpha * acc + pv
        m = m_new
        gmax = nxt
        yield
    l = jnp.sum(lrun, axis=0, keepdims=True)
    out_ref[b, pl.ds(pl.multiple_of(t * MOBA_BLOCK, MOBA_BLOCK), MOBA_BLOCK), :] = (acc / l).T.astype(out_ref.dtype)


def _moba_attn_kernel(qa_ref, ka_ref, vt_ref, bias_ref, out_ref, s_scr):
    c = pl.program_id(1)
    bsz = qa_ref.shape[0]
    nb = vt_ref.shape[2]
    for n_groups in range(1, nb // ATTN_GROUP + 1):
        @pl.when(c == n_groups - 1)
        def _():
            def tile(t, carry):
                j = c * ATTN_GROUP + t
                _run_staggered([_attn_tile(qa_ref, ka_ref, vt_ref, bias_ref, out_ref, s_scr, b, t, j, n_groups)
                                for b in range(bsz)])
                return carry
            lax.fori_loop(0, ATTN_GROUP, tile, 0)


def _moba_attn(qa, ka, vt, bias):
    bsz, seq, _ = ka.shape
    nb = seq // MOBA_BLOCK
    assert nb % ATTN_GROUP == 0
    width = 2 * HEAD_DIM
    return pl.pallas_call(
        _moba_attn_kernel,
        grid=(HEADS, nb // ATTN_GROUP),
        in_specs=[
            pl.BlockSpec((bsz, 1, ATTN_GROUP, width, MOBA_BLOCK), lambda h, c: (0, h, c, 0, 0)),
            pl.BlockSpec((bsz, seq, width), lambda h, c: (0, 0, h)),
            pl.BlockSpec((bsz, 1, nb, HEAD_DIM, MOBA_BLOCK), lambda h, c: (0, h, 0, 0, 0)),
            pl.BlockSpec((N_BIAS_TILES, 1, MOBA_BLOCK, MOBA_BLOCK), lambda h, c: (0, h, 0, 0)),
        ],
        out_specs=pl.BlockSpec((bsz, ATTN_GROUP * MOBA_BLOCK, HEAD_DIM), lambda h, c: (0, c, h)),
        out_shape=jax.ShapeDtypeStruct((bsz, seq, D_MODEL), BF16),
        scratch_shapes=[pltpu.VMEM((bsz, 2, ATTN_GROUP, MOBA_BLOCK, MOBA_BLOCK), F32)],
        compiler_params=pltpu.CompilerParams(
            dimension_semantics=("arbitrary", "arbitrary"), vmem_limit_bytes=VMEM_LIMIT),
        name="moba_attn",
    )(qa, ka, vt, bias)


def kernel(x, norm_mix, norm_ffn, hgrn_w_in, hgrn_lb_logits, hgrn_out_norm, hgrn_w_out,
           moba_w_in, moba_w_out, rel_bias_table, ffn_w13, ffn_w2, final_norm):
    depth = norm_mix.shape[0]
    n_mixers = 2
    hgrn_w_in, hgrn_w_out, moba_w_in, moba_w_out = (
        w.astype(BF16) for w in (hgrn_w_in, hgrn_w_out, moba_w_in, moba_w_out))
    norm_mix, norm_ffn, hgrn_out_norm = (g[:, None, :] for g in (norm_mix, norm_ffn, hgrn_out_norm))
    bias = _bias_tiles(rel_bias_table)
    h = x
    for layer in range(depth):
        idx = layer // n_mixers
        if layer % n_mixers == 0:
            a = _hgrn_mixer(h, norm_mix, layer, hgrn_w_in, hgrn_lb_logits, hgrn_out_norm, idx)
            w_out = hgrn_w_out
        else:
            qa, ka, vt = _moba_proj(h, norm_mix, layer, moba_w_in, idx, rel_bias_table)
            a = _moba_attn(qa, ka, vt, bias)
            w_out = moba_w_out
        h = _outproj_ffn(h, a, w_out, idx, norm_ffn, ffn_w13, ffn_w2, layer, final_norm,
                         final_norm=(layer == depth - 1))
    return h
```
